```python
import math
import jax, jax.numpy as jnp
from jax import lax
import numpy as np

D_MODEL = 2048
BATCH = 2
SEQ = 8192
DEPTH = 4

GLA_HEADS = 4
GLA_VDIM = D_MODEL // 2
GLA_KDIM = GLA_VDIM // 2
HEAD_K = GLA_KDIM // GLA_HEADS
HEAD_V = GLA_VDIM // GLA_HEADS
DECAY_RANK = 16
GATE_NORMALIZER = 16.0
CHUNK = 64
POOL_DIM = D_MODEL // 2
POOL_WINDOWS = (2, 4, 8, 16)
POOL_GROUPS = 4
POOL_GROUP_DIM = POOL_DIM // POOL_GROUPS
N_GROUPS = 8
EXPERTS_PER_GROUP = 8
N_EXPERTS = N_GROUPS * EXPERTS_PER_GROUP
TOP_K_FINE = 2
D_FF_EXPERT = D_MODEL // 8
MOE_BLOCK = 128
N_MOD = 6
EPS = 1e-6
IN_SPLITS = (GLA_KDIM, GLA_KDIM, GLA_VDIM, GLA_VDIM, 2 * DECAY_RANK, POOL_DIM, D_MODEL, D_MODEL)
IN_COLS = sum(IN_SPLITS)

kernel_name = 'hybrid_gla_pool_hmoe_encoder'


def rms_norm(x, g):
    x32 = x.astype(jnp.float32)
    y = x32 * lax.rsqrt(jnp.mean(x32 * x32, axis=-1, keepdims=True) + EPS)
    return y.astype(x.dtype) * g


def split_cols(t):
    out = []
    start = 0
    for w in IN_SPLITS:
        out.append(t[..., start:start + w])
        start += w
    return out


def to_heads(t, n_heads):
    b_, s_, w = t.shape
    return t.reshape(b_, s_, n_heads, w // n_heads).transpose(0, 2, 1, 3)


def gla_chunk_scan(q, k, v, log_a, strict):
    b_, h_, s_, dk = q.shape
    dv = v.shape[-1]
    n = s_ // CHUNK
    q = q.reshape(b_, h_, n, CHUNK, dk)
    k = k.reshape(b_, h_, n, CHUNK, dk)
    v = v.reshape(b_, h_, n, CHUNK, dv)
    cum = jnp.cumsum(log_a.reshape(b_, h_, n, CHUNK, dk), axis=3)
    last = cum[:, :, :, -1:, :]
    q_dec = q * jnp.exp(cum)
    k_inv = k * jnp.exp(-cum)
    k_end = k * jnp.exp(last - cum)
    mask = np.tril(np.ones((CHUNK, CHUNK), dtype=bool), k=-1 if strict else 0)
    att = jnp.einsum('bhncd,bhnsd->bhncs', q_dec, k_inv)
    att = jnp.where(mask, att, 0.0)
    o_intra = jnp.einsum('bhncs,bhnse->bhnce', att, v)
    chunk_decay = jnp.exp(last[:, :, :, 0, :])

    def step(state, inp):
        qd, ke, vv, dec = inp
        o = jnp.einsum('bhcd,bhde->bhce', qd, state)
        state = dec[..., None] * state + jnp.einsum('bhcd,bhce->bhde', ke, vv)
        return state, o

    xs = (jnp.moveaxis(q_dec, 2, 0), jnp.moveaxis(k_end, 2, 0),
          jnp.moveaxis(v, 2, 0), jnp.moveaxis(chunk_decay, 2, 0))
    state0 = jnp.zeros((b_, h_, dk, dv), jnp.float32)
    _, o_inter = lax.scan(step, state0, xs)
    o = o_intra + jnp.moveaxis(o_inter, 0, 2)
    return o.reshape(b_, h_, s_, dv)


def gla_mixer(q, k, v, r, lr, wf, bf, wb, bb, g_norm):
    dt = q.dtype
    f = lambda t: t.astype(jnp.float32)
    qh = to_heads(f(q), GLA_HEADS) * HEAD_K ** -0.5
    kh = to_heads(f(k), GLA_HEADS)
    vh = to_heads(f(v), GLA_HEADS)
    la_f = to_heads(jax.nn.log_sigmoid(f(lr[..., :DECAY_RANK]) @ f(wf) + f(bf)) / GATE_NORMALIZER, GLA_HEADS)
    la_b = to_heads(jax.nn.log_sigmoid(f(lr[..., DECAY_RANK:]) @ f(wb) + f(bb)) / GATE_NORMALIZER, GLA_HEADS)
    o_f = gla_chunk_scan(qh, kh, vh, la_f, strict=False)
    rev = lambda t: t[:, :, ::-1]
    o_b = rev(gla_chunk_scan(rev(qh), rev(kh), rev(vh), rev(la_b), strict=True))
    o = o_f + o_b
    o = o * lax.rsqrt(jnp.mean(o * o, axis=-1, keepdims=True) + EPS) * f(g_norm)
    b_, h_, s_, dv = o.shape
    o = o.transpose(0, 2, 1, 3).reshape(b_, s_, h_ * dv)
    return (o * jax.nn.silu(f(r))).astype(dt)


def multiscale_pool(u):
    b_, s_, g_, cg = u.shape
    cs = jnp.concatenate([jnp.zeros((b_, 1, g_, cg), u.dtype), jnp.cumsum(u, axis=1)], axis=1)
    t = jnp.arange(s_)
    outs = []
    for gi, w in enumerate(POOL_WINDOWS):
        lo = jnp.clip(t - w // 2, 0, s_)
        hi = jnp.clip(t + w // 2, 0, s_)
        cnt = (hi - lo).astype(u.dtype)
        csg = cs[:, :, gi]
        mean = (csg[:, hi] - csg[:, lo]) / cnt[None, :, None]
        outs.append(mean - u[:, :, gi])
    return jnp.stack(outs, axis=2)


def pool_mixer(u, w, b, scale):
    b_, s_, _ = u.shape
    u32 = u.astype(jnp.float32).reshape(b_, s_, POOL_GROUPS, POOL_GROUP_DIM)
    p = multiscale_pool(u32)
    y = jnp.einsum('bsgc,gcd->bsgd', p, w.astype(jnp.float32)).reshape(b_, s_, POOL_DIM)
    return ((y + b.astype(jnp.float32)) * scale.astype(jnp.float32)).astype(u.dtype)


def hier_moe(h, wc, bc, wf, bf, w1, w3, w2):
    b_, s_, d = h.shape
    t_ = b_ * s_
    hf = h.reshape(t_, d)
    coarse = (hf @ wc + bc).astype(jnp.float32)
    probs = jax.nn.softmax(coarse, axis=-1)
    g_star = jnp.argmax(coarse, axis=-1).astype(jnp.int32)
    p_g = jnp.take_along_axis(probs, g_star[:, None], axis=1)[:, 0]
    fine = (hf @ wf + bf).astype(jnp.float32).reshape(t_, N_GROUPS, EXPERTS_PER_GROUP)
    fine_sel = jnp.take_along_axis(fine, g_star[:, None, None], axis=1)[:, 0]
    top_v, top_i = lax.top_k(fine_sel, TOP_K_FINE)
    wts = jax.nn.softmax(top_v, axis=-1) * p_g[:, None]
    experts = g_star[:, None] * EXPERTS_PER_GROUP + top_i.astype(jnp.int32)
    n_assign = t_ * TOP_K_FINE
    n_blocks = -(-n_assign // MOE_BLOCK) + N_EXPERTS
    e_flat = experts.reshape(-1)
    w_flat = wts.reshape(-1)
    tok_flat = jnp.repeat(jnp.arange(t_, dtype=jnp.int32), TOP_K_FINE)
    counts = jnp.bincount(e_flat, length=N_EXPERTS)
    start = jnp.cumsum(counts) - counts
    padded = (counts + MOE_BLOCK - 1) // MOE_BLOCK * MOE_BLOCK
    pend = jnp.cumsum(padded)
    pstart = pend - padded
    order = jnp.argsort(e_flat, stable=True)
    e_sorted = e_flat[order]
    dest = pstart[e_sorted] + jnp.arange(n_assign) - start[e_sorted]
    n_slots = n_blocks * MOE_BLOCK
    slot_tok = jnp.full((n_slots,), t_, jnp.int32).at[dest].set(tok_flat[order])
    slot_w = jnp.zeros((n_slots,), h.dtype).at[dest].set(w_flat[order].astype(h.dtype))
    block_expert = jnp.minimum(
        jnp.searchsorted(pend // MOE_BLOCK, jnp.arange(n_blocks), side='right'), N_EXPERTS - 1)
    h_pad = jnp.concatenate([hf, jnp.zeros((1, d), h.dtype)], axis=0)
    xs = h_pad[slot_tok].reshape(n_blocks, MOE_BLOCK, d)

    def expert_block(args):
        e, xb = args
        return (jax.nn.silu(xb @ w1[e]) * (xb @ w3[e])) @ w2[e]

    ys = lax.map(expert_block, (block_expert, xs))
    y = jnp.zeros((t_ + 1, d), h.dtype).at[slot_tok].add(ys.reshape(-1, d) * slot_w[:, None])
    return y[:t_].reshape(b_, s_, d)


def setup_inputs(seed: int = 0) -> dict:
    key = jax.random.key(seed)
    ks = jax.random.split(key, 28)
    L, D = DEPTH, D_MODEL

    def nrm(k, shape, scale):
        return jax.random.normal(k, shape, jnp.float32) * scale

    return {
        'x': nrm(ks[0], (BATCH, SEQ, D), 1.0),
        'c': nrm(ks[1], (BATCH, D), 1.0),
        'ada_w': nrm(ks[2], (L, D, N_MOD * D), 0.5 * D ** -0.5),
        'ada_b': nrm(ks[3], (L, N_MOD * D), 0.02),
        'mix_norm_g': 1.0 + nrm(ks[4], (L, D), 0.02),
        'in_w': nrm(ks[5], (L, D, IN_COLS), D ** -0.5),
        'decay_fw_w': nrm(ks[6], (L, DECAY_RANK, GLA_KDIM), DECAY_RANK ** -0.5),
        'decay_fw_b': nrm(ks[7], (L, GLA_KDIM), 0.02),
        'decay_bw_w': nrm(ks[8], (L, DECAY_RANK, GLA_KDIM), DECAY_RANK ** -0.5),
        'decay_bw_b': nrm(ks[9], (L, GLA_KDIM), 0.02),
        'gla_norm_g': 1.0 + nrm(ks[10], (L, HEAD_V), 0.02),
        'pool_w': nrm(ks[11], (L, POOL_GROUPS, POOL_GROUP_DIM, POOL_GROUP_DIM), POOL_GROUP_DIM ** -0.5),
        'pool_b': nrm(ks[12], (L, POOL_DIM), 0.02),
        'pool_scale': 1.0 + nrm(ks[13], (L, POOL_DIM), 0.1),
        'branch_a_w': nrm(ks[14], (L, GLA_VDIM, D), GLA_VDIM ** -0.5),
        'branch_b_w': nrm(ks[15], (L, POOL_DIM, D), POOL_DIM ** -0.5),
        'out_w': nrm(ks[16], (L, D, D), D ** -0.5),
        'ffn_norm_g': 1.0 + nrm(ks[17], (L, D), 0.02),
        'router_coarse_w': nrm(ks[18], (L, D, N_GROUPS), D ** -0.5),
        'router_coarse_b': nrm(ks[19], (L, N_GROUPS), 0.01),
        'router_fine_w': nrm(ks[20], (L, D, N_EXPERTS), D ** -0.5),
        'router_fine_b': nrm(ks[21], (L, N_EXPERTS), 0.01),
        'expert_w1': nrm(ks[22], (L, N_EXPERTS, D, D_FF_EXPERT), D ** -0.5),
        'expert_w3': nrm(ks[23], (L, N_EXPERTS, D, D_FF_EXPERT), D ** -0.5),
        'expert_w2': nrm(ks[24], (L, N_EXPERTS, D_FF_EXPERT, D), D_FF_EXPERT ** -0.5),
        'final_norm_g': 1.0 + nrm(ks[25], (D,), 0.02),
    }


def reference(x, c, ada_w, ada_b, mix_norm_g, in_w, decay_fw_w, decay_fw_b, decay_bw_w, decay_bw_b,
              gla_norm_g, pool_w, pool_b, pool_scale, branch_a_w, branch_b_w, out_w, ffn_norm_g,
              router_coarse_w, router_coarse_b, router_fine_w, router_fine_b,
              expert_w1, expert_w3, expert_w2, final_norm_g):
    for l in range(DEPTH):
        mod = (jax.nn.silu(c) @ ada_w[l] + ada_b[l]).reshape(c.shape[0], N_MOD, D_MODEL)[:, :, None, :]
        shift1, scale1, gate1, shift2, scale2, gate2 = [mod[:, i] for i in range(N_MOD)]
        h = rms_norm(x, mix_norm_g[l]) * (1.0 + scale1) + shift1
        q, k, v, r, lr, u, ga, gb = split_cols(h @ in_w[l])
        y_a = gla_mixer(q, k, v, r, lr, decay_fw_w[l], decay_fw_b[l], decay_bw_w[l], decay_bw_b[l],
                        gla_norm_g[l]) @ branch_a_w[l]
        y_b = pool_mixer(u, pool_w[l], pool_b[l], pool_scale[l]) @ branch_b_w[l]
        merged = jax.nn.sigmoid(ga) * y_a + jax.nn.sigmoid(gb) * y_b
        x = x + gate1 * (merged @ out_w[l])
        h = rms_norm(x, ffn_norm_g[l]) * (1.0 + scale2) + shift2
        x = x + gate2 * hier_moe(h, router_coarse_w[l], router_coarse_b[l], router_fine_w[l],
                                 router_fine_b[l], expert_w1[l], expert_w3[l], expert_w2[l])
    return rms_norm(x, final_norm_g)
```

```python
import functools

import jax
import jax.numpy as jnp
from jax import lax
from jax.experimental import pallas as pl
from jax.experimental.pallas import tpu as pltpu

F32 = jnp.float32
BF16 = jnp.bfloat16

D_MODEL = 2048
GLA_HEADS = 4
GLA_VDIM = 1024
GLA_KDIM = 512
HEAD_K = 128
HEAD_V = 256
DECAY_RANK = 16
GATE_NORMALIZER = 16.0
CHUNK = 64
POOL_DIM = 1024
POOL_WINDOWS = (2, 4, 8, 16)
POOL_GROUP_DIM = 256
POOL_HALO = 16
N_GROUPS = 8
EXPERTS_PER_GROUP = 8
N_EXPERTS = 64
TOP_K_FINE = 2
D_FF_EXPERT = 256
MOE_BLOCK = 128
N_MOD = 6
EPS = 1e-6
LANES = 128
MAIN_COLS = 8192
VMEM_LIMIT = 56 * 1024 * 1024


def _cparams(sem):
    return pltpu.CompilerParams(dimension_semantics=sem, vmem_limit_bytes=VMEM_LIMIT)


def _split2(a):
    hi = a.astype(BF16)
    lo = (a - hi.astype(F32)).astype(BF16)
    return hi, lo


def _split3(a):
    p1 = a.astype(BF16)
    r1 = a - p1.astype(F32)
    p2 = r1.astype(BF16)
    p3 = (r1 - p2.astype(F32)).astype(BF16)
    return p1, p2, p3


def _dot(a, b):
    return jnp.dot(a, b, preferred_element_type=F32)


def _dot3(a, b):
    ah, al = _split2(a)
    bh, bl = _split2(b)
    return _dot(ah, bh) + (_dot(ah, bl) + _dot(al, bh))


def _ada_kernel(c_ref, w_ref, b_ref, o_ref):
    c = c_ref[...]
    s = (c * jax.nn.sigmoid(c)).astype(BF16)
    o_ref[0] = _dot(s, w_ref[0].astype(BF16)) + b_ref[0]


def _ada_mod(c, ada_w, ada_b):
    n_layers, d, n = ada_w.shape
    bsz = c.shape[0]
    rows = 8
    tn = 1024
    c_pad = jnp.zeros((rows, d), F32).at[:bsz].set(c)
    out = pl.pallas_call(
        _ada_kernel,
        grid=(n_layers, n // tn),
        in_specs=[
            pl.BlockSpec((rows, d), lambda l, j: (0, 0)),
            pl.BlockSpec((1, d, tn), lambda l, j: (l, 0, j)),
            pl.BlockSpec((1, 1, tn), lambda l, j: (l, 0, j)),
        ],
        out_specs=pl.BlockSpec((1, rows, tn), lambda l, j: (l, 0, j)),
        out_shape=jax.ShapeDtypeStruct((n_layers, rows, n), F32),
        compiler_params=_cparams(("parallel", "parallel")),
        name="ada_mod",
    )(c_pad, ada_w, ada_b.reshape(n_layers, 1, n))
    return out[:, :bsz].reshape(n_layers, bsz, N_MOD, 1, d)


def _inproj_kernel(x_ref, g_ref, sc_ref, sh_ref, w_ref, wlr_ref, o_ref, lr_ref, h_scr):
    @pl.when(pl.program_id(1) == 0)
    def _():
        x = x_ref[...]
        ms = jnp.mean(x * x, axis=-1, keepdims=True)
        y = x * lax.rsqrt(ms + EPS) * g_ref[...]
        hb = (y * (1.0 + sc_ref[0]) + sh_ref[0]).astype(BF16)
        h_scr[...] = hb
        lr_ref[...] = _dot(hb, wlr_ref[...])

    o_ref[...] = _dot(h_scr[...], w_ref[...]).astype(o_ref.dtype)


def _in_proj(x2, g, scale, shift, w_main, w_lr, seq):
    t, d = x2.shape
    tm = min(1024, seq)
    tn = 1024
    tiles_per_seq = seq // tm
    return pl.pallas_call(
        _inproj_kernel,
        grid=(t // tm, MAIN_COLS // tn),
        in_specs=[
            pl.BlockSpec((tm, d), lambda i, j: (i, 0)),
            pl.BlockSpec((1, d), lambda i, j: (0, 0)),
            pl.BlockSpec((1, 1, d), lambda i, j: (i // tiles_per_seq, 0, 0)),
            pl.BlockSpec((1, 1, d), lambda i, j: (i // tiles_per_seq, 0, 0)),
            pl.BlockSpec((d, tn), lambda i, j: (0, j)),
            pl.BlockSpec((d, LANES), lambda i, j: (0, 0)),
        ],
        out_specs=[
            pl.BlockSpec((tm, tn), lambda i, j: (i, j)),
            pl.BlockSpec((tm, LANES), lambda i, j: (i, 0)),
        ],
        out_shape=[
            jax.ShapeDtypeStruct((t, MAIN_COLS), BF16),
            jax.ShapeDtypeStruct((t, LANES), F32),
        ],
        scratch_shapes=[pltpu.VMEM((tm, d), BF16)],
        compiler_params=_cparams(("parallel", "arbitrary")),
        name="in_proj",
    )(x2, g.reshape(1, d), scale, shift, w_main, w_lr)


def _gla_kernel(*refs, reverse, bt):
    if reverse:
        q_ref, k_ref, v_ref, lr_ref, wd_ref, bd_ref, o_ref, st_ref = refs
    else:
        (q_ref, k_ref, v_ref, lr_ref, wd_ref, bd_ref, ob_ref, r_ref, gn_ref,
         o_ref, st_ref, o_scr) = refs

    @pl.when(pl.program_id(1) == 0)
    def _():
        st_ref[...] = jnp.zeros_like(st_ref)

    z = _dot3(lr_ref[...], wd_ref[...]) + bd_ref[...]
    la = (jnp.minimum(z, 0.0) - jnp.log1p(jnp.exp(-jnp.abs(z)))) * (1.0 / GATE_NORMALIZER)

    rows = lax.broadcasted_iota(jnp.int32, (bt, bt), 0)
    cols = lax.broadcasted_iota(jnp.int32, (bt, bt), 1)
    same_chunk = (rows // CHUNK) == (cols // CHUNK)
    ordered = (cols >= rows) if reverse else (cols <= rows)
    tri = jnp.where(same_chunk & ordered, 1.0, 0.0).astype(BF16)
    l1, l2, l3 = _split3(la)
    cum = _dot(tri, l1) + (_dot(tri, l2) + _dot(tri, l3))

    crow = lax.broadcasted_iota(jnp.int32, (CHUNK, CHUNK), 0)
    ccol = lax.broadcasted_iota(jnp.int32, (CHUNK, CHUNK), 1)
    att_mask = (ccol > crow) if reverse else (ccol <= crow)

    n_chunks = bt // CHUNK
    chunk_order = range(n_chunks - 1, -1, -1) if reverse else range(n_chunks)
    q_scale = HEAD_K ** -0.5
    for h in range(GLA_HEADS):
        hk = slice(h * HEAD_K, (h + 1) * HEAD_K)
        hv = slice(h * HEAD_V, (h + 1) * HEAD_V)
        state = st_ref[h]
        for c in chunk_order:
            sl = slice(c * CHUNK, (c + 1) * CHUNK)
            cm = cum[sl, hk]
            last = cm[0:1] if reverse else cm[CHUNK - 1:CHUNK]
            qf = q_ref[sl, hk].astype(F32) * q_scale
            kf = k_ref[sl, hk].astype(F32)
            vb = v_ref[sl, hv]
            q_dec = (qf * jnp.exp(cm)).astype(BF16)
            k_inv = (kf * jnp.exp(-cm)).astype(BF16)
            k_end = (kf * jnp.exp(last - cm)).astype(BF16)
            att = lax.dot_general(q_dec, k_inv, (((1,), (1,)), ((), ())), preferred_element_type=F32)
            att = jnp.where(att_mask, att, 0.0).astype(BF16)
            o = _dot(att, vb) + _dot(q_dec, state.astype(BF16))
            if reverse:
                o_ref[sl, hv] = o
            else:
                o_scr[sl, hv] = o
            dec_col = jnp.transpose(jnp.broadcast_to(jnp.exp(last), (HEAD_K, HEAD_K)))
            dec = jnp.concatenate([dec_col, dec_col], axis=1)
            kv = lax.dot_general(k_end, vb, (((0,), (0,)), ((), ())), preferred_element_type=F32)
            state = dec * state + kv
        st_ref[h] = state

    if not reverse:
        for h in range(GLA_HEADS):
            hv = slice(h * HEAD_V, (h + 1) * HEAD_V)
            o = o_scr[:, hv] + ob_ref[:, hv]
            ms = jnp.mean(o * o, axis=-1, keepdims=True)
            y = o * lax.rsqrt(ms + EPS) * gn_ref[...]
            r = r_ref[:, hv].astype(F32)
            o_ref[:, hv] = (y * (r * jax.nn.sigmoid(r))).astype(o_ref.dtype)


def _gla_scan(main, lr, wdec, bdec, seq, reverse, o_bwd=None, g_norm=None):
    t = main.shape[0]
    bsz = t // seq
    bt = min(256, seq)
    nb = seq // bt

    def row(b, n):
        return b * nb + ((nb - 1 - n) if reverse else n)

    in_specs = [
        pl.BlockSpec((bt, GLA_KDIM), lambda b, n: (row(b, n), 0)),
        pl.BlockSpec((bt, GLA_KDIM), lambda b, n: (row(b, n), 1)),
        pl.BlockSpec((bt, GLA_VDIM), lambda b, n: (row(b, n), 1)),
        pl.BlockSpec((bt, LANES), lambda b, n: (row(b, n), 0)),
        pl.BlockSpec((LANES, GLA_KDIM), lambda b, n: (0, 0)),
        pl.BlockSpec((1, GLA_KDIM), lambda b, n: (0, 0)),
    ]
    args = [main, main, main, lr, wdec, bdec]
    scratch = [pltpu.VMEM((GLA_HEADS, HEAD_K, HEAD_V), F32)]
    if reverse:
        out_dtype = F32
    else:
        in_specs += [
            pl.BlockSpec((bt, GLA_VDIM), lambda b, n: (row(b, n), 0)),
            pl.BlockSpec((bt, GLA_VDIM), lambda b, n: (row(b, n), 2)),
            pl.BlockSpec((1, HEAD_V), lambda b, n: (0, 0)),
        ]
        args += [o_bwd, main, g_norm.reshape(1, HEAD_V)]
        scratch.append(pltpu.VMEM((bt, GLA_VDIM), F32))
        out_dtype = BF16
    return pl.pallas_call(
        functools.partial(_gla_kernel, reverse=reverse, bt=bt),
        grid=(bsz, nb),
        in_specs=in_specs,
        out_specs=pl.BlockSpec((bt, GLA_VDIM), lambda b, n: (row(b, n), 0)),
        out_shape=jax.ShapeDtypeStruct((t, GLA_VDIM), out_dtype),
        scratch_shapes=scratch,
        compiler_params=_cparams(("parallel", "arbitrary")),
        name="gla_bwd" if reverse else "gla_fwd",
    )(*args)


def _pool_kernel(up_ref, uc_ref, un_ref, w_ref, b_ref, s_ref, o_ref, ext_ref, *, tm, tiles_per_seq, seq):
    it = pl.program_id(0) % tiles_per_seq
    prev = up_ref[...].astype(F32)
    nxt = un_ref[...].astype(F32)
    ext_ref[0:POOL_HALO, :] = jnp.where(it == 0, 0.0, prev)
    ext_ref[POOL_HALO:POOL_HALO + tm, :] = uc_ref[...].astype(F32)
    ext_ref[POOL_HALO + tm:, :] = jnp.where(it == tiles_per_seq - 1, 0.0, nxt)
    pos = it * tm + lax.broadcasted_iota(jnp.int32, (tm, POOL_GROUP_DIM), 0)
    for gi, w in enumerate(POOL_WINDOWS):
        cs = slice(gi * POOL_GROUP_DIM, (gi + 1) * POOL_GROUP_DIM)
        start = POOL_HALO - w // 2
        acc = ext_ref[start:start + tm, cs]
        for j in range(1, w):
            acc = acc + ext_ref[start + j:start + j + tm, cs]
        cnt = jnp.minimum(pos + w // 2, seq) - jnp.maximum(pos - w // 2, 0)
        p = acc / cnt.astype(F32) - ext_ref[POOL_HALO:POOL_HALO + tm, cs]
        y = _dot(p.astype(BF16), w_ref[gi])
        o_ref[:, cs] = ((y + b_ref[:, cs]) * s_ref[:, cs]).astype(o_ref.dtype)


def _pool_mixer(main, w, b, scale, seq):
    t = main.shape[0]
    tm = min(512, seq)
    tiles_per_seq = seq // tm
    hb = tm // POOL_HALO
    n_hblocks = t // POOL_HALO
    u_col = 3
    return pl.pallas_call(
        functools.partial(_pool_kernel, tm=tm, tiles_per_seq=tiles_per_seq, seq=seq),
        grid=(t // tm,),
        in_specs=[
            pl.BlockSpec((POOL_HALO, POOL_DIM), lambda i: (jnp.maximum(i * hb - 1, 0), u_col)),
            pl.BlockSpec((tm, POOL_DIM), lambda i: (i, u_col)),
            pl.BlockSpec((POOL_HALO, POOL_DIM), lambda i: (jnp.minimum((i + 1) * hb, n_hblocks - 1), u_col)),
            pl.BlockSpec((len(POOL_WINDOWS), POOL_GROUP_DIM, POOL_GROUP_DIM), lambda i: (0, 0, 0)),
            pl.BlockSpec((1, POOL_DIM), lambda i: (0, 0)),
            pl.BlockSpec((1, POOL_DIM), lambda i: (0, 0)),
        ],
        out_specs=pl.BlockSpec((tm, POOL_DIM), lambda i: (i, 0)),
        out_shape=jax.ShapeDtypeStruct((t, POOL_DIM), BF16),
        scratch_shapes=[pltpu.VMEM((tm + 2 * POOL_HALO, POOL_DIM), F32)],
        compiler_params=_cparams(("parallel",)),
        name="pool_mixer",
    )(main, main, main, w, b.reshape(1, POOL_DIM), scale.reshape(1, POOL_DIM))


def _merge_kernel(a_ref, b_ref, ga_ref, gb_ref, x_ref, wa_ref, wb_ref, wo_ref, g1_ref,
                  fg_ref, sc_ref, sh_ref, wrh_ref, wrl_ref, br_ref, xo_ref, h_ref, rt_ref):
    ya = _dot(a_ref[...], wa_ref[...])
    yb = _dot(b_ref[...], wb_ref[...])
    merged = jax.nn.sigmoid(ga_ref[...].astype(F32)) * ya + jax.nn.sigmoid(gb_ref[...].astype(F32)) * yb
    xn = x_ref[...] + g1_ref[0] * _dot(merged.astype(BF16), wo_ref[...])
    xo_ref[...] = xn
    ms = jnp.mean(xn * xn, axis=-1, keepdims=True)
    h = xn * lax.rsqrt(ms + EPS) * fg_ref[...] * (1.0 + sc_ref[0]) + sh_ref[0]
    h_ref[...] = h

    hh, hl = _split2(h)
    lg = _dot(hh, wrh_ref[...]) + (_dot(hh, wrl_ref[...]) + _dot(hl, wrh_ref[...])) + br_ref[...]
    lane = lax.broadcasted_iota(jnp.int32, lg.shape, 1).astype(F32)
    neg = -jnp.inf
    far = float(LANES)
    coarse = lane < N_GROUPS
    cmax = jnp.max(jnp.where(coarse, lg, neg), axis=-1, keepdims=True)
    g_star = jnp.min(jnp.where(coarse & (lg == cmax), lane, far), axis=-1, keepdims=True)
    p_g = 1.0 / jnp.sum(jnp.where(coarse, jnp.exp(lg - cmax), 0.0), axis=-1, keepdims=True)
    lo = N_GROUPS + EXPERTS_PER_GROUP * g_star
    fine = (lane >= lo) & (lane < lo + EXPERTS_PER_GROUP)
    v1 = jnp.max(jnp.where(fine, lg, neg), axis=-1, keepdims=True)
    i1 = jnp.min(jnp.where(fine & (lg == v1), lane, far), axis=-1, keepdims=True)
    fine2 = fine & (lane != i1)
    v2 = jnp.max(jnp.where(fine2, lg, neg), axis=-1, keepdims=True)
    i2 = jnp.min(jnp.where(fine2 & (lg == v2), lane, far), axis=-1, keepdims=True)
    e = jnp.exp(v2 - v1)
    w1 = p_g / (1.0 + e)
    w2 = p_g * e / (1.0 + e)
    rt = jnp.where(lane == 0, i1 - N_GROUPS,
                   jnp.where(lane == 1, i2 - N_GROUPS,
                             jnp.where(lane == 2, w1, jnp.where(lane == 3, w2, 0.0))))
    rt_ref[...] = rt


def _merge(a_in, b_in, main, x2, wa, wb, wo, gate1, fg, scale2, shift2, wr_hi, wr_lo, br, seq):
    t, d = x2.shape
    tm = min(256, seq)
    tiles_per_seq = seq // tm
    const = dict(pipeline_mode=pl.Buffered(1))
    bidx = lambda i: (i // tiles_per_seq, 0, 0)
    return pl.pallas_call(
        _merge_kernel,
        grid=(t // tm,),
        in_specs=[
            pl.BlockSpec((tm, GLA_VDIM), lambda i: (i, 0)),
            pl.BlockSpec((tm, POOL_DIM), lambda i: (i, 0)),
            pl.BlockSpec((tm, d), lambda i: (i, 2)),
            pl.BlockSpec((tm, d), lambda i: (i, 3)),
            pl.BlockSpec((tm, d), lambda i: (i, 0)),
            pl.BlockSpec((GLA_VDIM, d), lambda i: (0, 0), **const),
            pl.BlockSpec((POOL_DIM, d), lambda i: (0, 0), **const),
            pl.BlockSpec((d, d), lambda i: (0, 0), **const),
            pl.BlockSpec((1, 1, d), bidx),
            pl.BlockSpec((1, d), lambda i: (0, 0)),
            pl.BlockSpec((1, 1, d), bidx),
            pl.BlockSpec((1, 1, d), bidx),
            pl.BlockSpec((d, LANES), lambda i: (0, 0), **const),
            pl.BlockSpec((d, LANES), lambda i: (0, 0), **const),
            pl.BlockSpec((1, LANES), lambda i: (0, 0)),
        ],
        out_specs=[
            pl.BlockSpec((tm, d), lambda i: (i, 0)),
            pl.BlockSpec((tm, d), lambda i: (i, 0)),
            pl.BlockSpec((tm, LANES), lambda i: (i, 0)),
        ],
        out_shape=[
            jax.ShapeDtypeStruct((t, d), F32),
            jax.ShapeDtypeStruct((t, d), F32),
            jax.ShapeDtypeStruct((t, LANES), F32),
        ],
        compiler_params=_cparams(("parallel",)),
        name="merge",
    )(a_in, b_in, main, main, x2, wa, wb, wo, gate1, fg.reshape(1, d), scale2, shift2, wr_hi, wr_lo, br)


def _moe_kernel(be_ref, nv_ref, stok_ref, sdst_ref, h_hbm, w1_ref, w3_ref, w2_ref, y_hbm,
                xbuf, ybuf, w1b, w3b, w2b, wview, gsem, ssem, *, n_blocks):
    b = pl.program_id(0)
    slot = b % 2

    def gather_copy(sl, src_rows, dst_rows):
        return pltpu.make_async_copy(h_hbm.at[src_rows], xbuf.at[sl, dst_rows], gsem.at[sl])

    def scatter_copy(sl, src_rows, dst_rows):
        return pltpu.make_async_copy(ybuf.at[sl, src_rows], y_hbm.at[dst_rows], ssem.at[sl])

    def issue_gather(blk, sl):
        def body(i, carry):
            tok = stok_ref[blk * MOE_BLOCK + i]
            gather_copy(sl, pl.ds(tok, 1), pl.ds(i, 1)).start()
            return carry
        lax.fori_loop(0, nv_ref[blk], body, 0)

    def issue_scatter(blk, sl):
        def body(i, carry):
            dst = sdst_ref[blk * MOE_BLOCK + i]
            scatter_copy(sl, pl.ds(i, 1), pl.ds(dst, 1)).start()
            return carry
        lax.fori_loop(0, nv_ref[blk], body, 0)

    def wait_rows(sem, n):
        pltpu.make_async_copy(wview.at[pl.ds(0, n)], wview.at[pl.ds(0, n)], sem).wait()

    def wait_gather(blk, sl):
        wait_rows(gsem.at[sl], nv_ref[blk])

    def wait_scatter(blk, sl):
        wait_rows(ssem.at[sl], nv_ref[blk])

    @pl.when(b == 0)
    def _():
        xbuf[...] = jnp.zeros_like(xbuf)
        issue_gather(0, 0)

    @pl.when(b + 1 < n_blocks)
    def _():
        issue_gather(b + 1, 1 - slot)

    @pl.when((b >= 2) & (nv_ref[jnp.maximum(b - 2, 0)] > 0))
    def _():
        wait_scatter(b - 2, slot)

    @pl.when(nv_ref[b] > 0)
    def _():
        prev_e = be_ref[jnp.maximum(b - 1, 0)]

        @pl.when((b == 0) | (be_ref[b] != prev_e))
        def _():
            w1b[...] = w1_ref[0, 0].astype(BF16)
            w3b[...] = w3_ref[0, 0].astype(BF16)
            w2b[...] = w2_ref[0, 0].astype(BF16)

        wait_gather(b, slot)
        xb = xbuf[slot].astype(BF16)
        a = _dot(xb, w1b[...])
        g = _dot(xb, w3b[...])
        mid = (a * jax.nn.sigmoid(a) * g).astype(BF16)
        ybuf[slot] = _dot(mid, w2b[...])
        issue_scatter(b, slot)

    @pl.when(b == n_blocks - 1)
    def _():
        @pl.when((b >= 1) & (nv_ref[jnp.maximum(b - 1, 0)] > 0))
        def _():
            wait_scatter(b - 1, 1 - slot)

        @pl.when(nv_ref[b] > 0)
        def _():
            wait_scatter(b, slot)


def _moe_experts(h2, block_expert, n_valid, slot_tok, slot_dst, w1, w3, w2, layer):
    t, d = h2.shape
    n_blocks = block_expert.shape[0]
    f = w1.shape[-1]
    grid_spec = pltpu.PrefetchScalarGridSpec(
        num_scalar_prefetch=4,
        grid=(n_blocks,),
        in_specs=[
            pl.BlockSpec(memory_space=pl.ANY),
            pl.BlockSpec((1, 1, d, f), lambda b, be, nu, st, sd: (layer, be[b], 0, 0)),
            pl.BlockSpec((1, 1, d, f), lambda b, be, nu, st, sd: (layer, be[b], 0, 0)),
            pl.BlockSpec((1, 1, f, d), lambda b, be, nu, st, sd: (layer, be[b], 0, 0)),
        ],
        out_specs=pl.BlockSpec(memory_space=pl.ANY),
        scratch_shapes=[
            pltpu.VMEM((2, MOE_BLOCK, d), F32),
            pltpu.VMEM((2, MOE_BLOCK, d), F32),
            pltpu.VMEM((d, f), BF16),
            pltpu.VMEM((d, f), BF16),
            pltpu.VMEM((f, d), BF16),
            pltpu.VMEM((MOE_BLOCK, d // LANES, LANES), F32),
            pltpu.SemaphoreType.DMA((2,)),
            pltpu.SemaphoreType.DMA((2,)),
        ],
    )
    return pl.pallas_call(
        functools.partial(_moe_kernel, n_blocks=n_blocks),
        grid_spec=grid_spec,
        out_shape=jax.ShapeDtypeStruct((TOP_K_FINE * t, d), F32),
        compiler_params=_cparams(("arbitrary",)),
        name="moe_experts",
    )(block_expert, n_valid, slot_tok, slot_dst, h2, w1, w3, w2)


def _route_plan(route, t):
    experts = route[:, :TOP_K_FINE].astype(jnp.int32)
    n_assign = t * TOP_K_FINE
    n_blocks = -(-n_assign // MOE_BLOCK) + N_EXPERTS
    e_flat = experts.reshape(-1)
    a_idx = jnp.arange(n_assign, dtype=jnp.int32)
    tok_flat = a_idx // TOP_K_FINE
    dst_flat = (a_idx % TOP_K_FINE) * t + tok_flat
    counts = jnp.bincount(e_flat, length=N_EXPERTS)
    start = jnp.cumsum(counts) - counts
    padded = (counts + MOE_BLOCK - 1) // MOE_BLOCK * MOE_BLOCK
    pend = jnp.cumsum(padded)
    pstart = pend - padded
    order = jnp.argsort(e_flat, stable=True)
    e_sorted = e_flat[order]
    dest = pstart[e_sorted] + a_idx - start[e_sorted]
    n_slots = n_blocks * MOE_BLOCK
    slot_tok = jnp.zeros((n_slots,), jnp.int32).at[dest].set(tok_flat[order])
    slot_dst = jnp.zeros((n_slots,), jnp.int32).at[dest].set(dst_flat[order])
    blocks = jnp.arange(n_blocks)
    block_expert = jnp.minimum(jnp.searchsorted(pend // MOE_BLOCK, blocks, side='right'), N_EXPERTS - 1)
    in_expert = blocks - pstart[block_expert] // MOE_BLOCK
    n_valid = jnp.clip(counts[block_expert] - in_expert * MOE_BLOCK, 0, MOE_BLOCK)
    return block_expert.astype(jnp.int32), n_valid.astype(jnp.int32), slot_tok, slot_dst


def _combine_kernel(x_ref, y0_ref, y1_ref, rt_ref, g2_ref, fg_ref, o_ref, *, final):
    rt = rt_ref[...]
    w0 = rt[:, 2:3]
    w1 = rt[:, 3:4]
    xn = x_ref[...] + g2_ref[0] * (w0 * y0_ref[...] + w1 * y1_ref[...])
    if final:
        ms = jnp.mean(xn * xn, axis=-1, keepdims=True)
        xn = xn * lax.rsqrt(ms + EPS) * fg_ref[...]
    o_ref[...] = xn


def _combine(x2, y, route, gate2, final_g, seq, final):
    t, d = x2.shape
    tm = min(256, seq)
    tiles_per_seq = seq // tm
    n_tiles = t // tm
    return pl.pallas_call(
        functools.partial(_combine_kernel, final=final),
        grid=(n_tiles,),
        in_specs=[
            pl.BlockSpec((tm, d), lambda i: (i, 0)),
            pl.BlockSpec((tm, d), lambda i: (i, 0)),
            pl.BlockSpec((tm, d), lambda i: (i + n_tiles, 0)),
            pl.BlockSpec((tm, LANES), lambda i: (i, 0)),
            pl.BlockSpec((1, 1, d), lambda i: (i // tiles_per_seq, 0, 0)),
            pl.BlockSpec((1, d), lambda i: (0, 0)),
        ],
        out_specs=pl.BlockSpec((tm, d), lambda i: (i, 0)),
        out_shape=jax.ShapeDtypeStruct((t, d), F32),
        compiler_params=_cparams(("parallel",)),
        name="combine_final" if final else "combine",
    )(x2, y, y, route, gate2, final_g.reshape(1, d))


def _reorder_in_w(in_w_l):
    lr0 = 2 * GLA_KDIM + 2 * GLA_VDIM
    lr1 = lr0 + 2 * DECAY_RANK
    w_main = jnp.concatenate([in_w_l[:, :lr0], in_w_l[:, lr1:]], axis=1).astype(BF16)
    w_lr = jnp.zeros((in_w_l.shape[0], LANES), BF16).at[:, :2 * DECAY_RANK].set(in_w_l[:, lr0:lr1].astype(BF16))
    return w_main, w_lr


def kernel(x, c, ada_w, ada_b, mix_norm_g, in_w, decay_fw_w, decay_fw_b, decay_bw_w, decay_bw_b, gla_norm_g, pool_w, pool_b, pool_scale, branch_a_w, branch_b_w, out_w, ffn_norm_g, router_coarse_w, router_coarse_b, router_fine_w, router_fine_b, expert_w1, expert_w3, expert_w2, final_norm_g):
    bsz, seq, d = x.shape
    t = bsz * seq
    n_layers = ada_w.shape[0]
    mod = _ada_mod(c, ada_w, ada_b)
    x2 = x.reshape(t, d)
    for l in range(n_layers):
        shift1, scale1, gate1, shift2, scale2, gate2 = [mod[l, :, i] for i in range(N_MOD)]
        w_main, w_lr = _reorder_in_w(in_w[l])
        main, lr = _in_proj(x2, mix_norm_g[l], scale1, shift1, w_main, w_lr, seq)

        wdec_f = jnp.zeros((LANES, GLA_KDIM), F32).at[:DECAY_RANK].set(decay_fw_w[l])
        wdec_b = jnp.zeros((LANES, GLA_KDIM), F32).at[DECAY_RANK:2 * DECAY_RANK].set(decay_bw_w[l])
        o_bwd = _gla_scan(main, lr, wdec_b, decay_bw_b[l].reshape(1, GLA_KDIM), seq, reverse=True)
        a_in = _gla_scan(main, lr, wdec_f, decay_fw_b[l].reshape(1, GLA_KDIM), seq, reverse=False,
                         o_bwd=o_bwd, g_norm=gla_norm_g[l])
        b_in = _pool_mixer(main, pool_w[l].astype(BF16), pool_b[l], pool_scale[l], seq)

        wr = jnp.zeros((d, LANES), F32)
        wr = wr.at[:, :N_GROUPS].set(router_coarse_w[l]).at[:, N_GROUPS:N_GROUPS + N_EXPERTS].set(router_fine_w[l])
        wr_hi = wr.astype(BF16)
        wr_lo = (wr - wr_hi.astype(F32)).astype(BF16)
        br = jnp.zeros((1, LANES), F32)
        br = br.at[0, :N_GROUPS].set(router_coarse_b[l]).at[0, N_GROUPS:N_GROUPS + N_EXPERTS].set(router_fine_b[l])
        x2, h2, route = _merge(a_in, b_in, main, x2, branch_a_w[l].astype(BF16), branch_b_w[l].astype(BF16),
                               out_w[l].astype(BF16), gate1, ffn_norm_g[l], scale2, shift2, wr_hi, wr_lo, br, seq)

        block_expert, n_valid, slot_tok, slot_dst = _route_plan(route, t)
        y = _moe_experts(h2, block_expert, n_valid, slot_tok, slot_dst, expert_w1, expert_w3, expert_w2, l)
        x2 = _combine(x2, y, route, gate2, final_norm_g, seq, final=(l == n_layers - 1))
    return x2.reshape(bsz, seq, d)
```

```python
import functools

import jax
import jax.numpy as jnp
from jax import lax
from jax.experimental import pallas as pl
from jax.experimental.pallas import tpu as pltpu

F32 = jnp.float32
BF16 = jnp.bfloat16

D_MODEL = 2048
GLA_HEADS = 4
GLA_VDIM = 1024
GLA_KDIM = 512
HEAD_K = 128
HEAD_V = 256
DECAY_RANK = 16
GATE_NORMALIZER = 16.0
CHUNK = 64
POOL_DIM = 1024
POOL_WINDOWS = (2, 4, 8, 16)
POOL_GROUP_DIM = 256
POOL_HALO = 16
N_GROUPS = 8
EXPERTS_PER_GROUP = 8
N_EXPERTS = 64
TOP_K_FINE = 2
D_FF_EXPERT = 256
MOE_BLOCK = 128
ISSUE_UNROLL = 8
N_MOD = 6
EPS = 1e-6
LANES = 128
MAIN_COLS = 8192
VMEM_LIMIT = 56 * 1024 * 1024


def _cparams(sem):
    return pltpu.CompilerParams(dimension_semantics=sem, vmem_limit_bytes=VMEM_LIMIT)


def _split2(a):
    hi = a.astype(BF16)
    lo = (a - hi.astype(F32)).astype(BF16)
    return hi, lo


def _split3(a):
    p1 = a.astype(BF16)
    r1 = a - p1.astype(F32)
    p2 = r1.astype(BF16)
    p3 = (r1 - p2.astype(F32)).astype(BF16)
    return p1, p2, p3


def _dot(a, b):
    return jnp.dot(a, b, preferred_element_type=F32)


def _dot3(a, b):
    ah, al = _split2(a)
    bh, bl = _split2(b)
    return _dot(ah, bh) + (_dot(ah, bl) + _dot(al, bh))


def _ada_kernel(c_ref, w_ref, b_ref, o_ref):
    c = c_ref[...]
    s = (c * jax.nn.sigmoid(c)).astype(BF16)
    o_ref[0] = _dot(s, w_ref[0].astype(BF16)) + b_ref[0]


def _ada_mod(c, ada_w, ada_b):
    n_layers, d, n = ada_w.shape
    bsz = c.shape[0]
    rows = 8
    tn = 1024
    c_pad = jnp.zeros((rows, d), F32).at[:bsz].set(c)
    out = pl.pallas_call(
        _ada_kernel,
        grid=(n_layers, n // tn),
        in_specs=[
            pl.BlockSpec((rows, d), lambda l, j: (0, 0)),
            pl.BlockSpec((1, d, tn), lambda l, j: (l, 0, j)),
            pl.BlockSpec((1, 1, tn), lambda l, j: (l, 0, j)),
        ],
        out_specs=pl.BlockSpec((1, rows, tn), lambda l, j: (l, 0, j)),
        out_shape=jax.ShapeDtypeStruct((n_layers, rows, n), F32),
        compiler_params=_cparams(("parallel", "parallel")),
        name="ada_mod",
    )(c_pad, ada_w, ada_b.reshape(n_layers, 1, n))
    return out[:, :bsz].reshape(n_layers, bsz, N_MOD, 1, d)


def _inproj_kernel(x_ref, g_ref, sc_ref, sh_ref, w_ref, wlr_ref, o_ref, lr_ref, h_scr):
    @pl.when(pl.program_id(1) == 0)
    def _():
        x = x_ref[...]
        ms = jnp.mean(x * x, axis=-1, keepdims=True)
        y = x * lax.rsqrt(ms + EPS) * g_ref[...]
        hb = (y * (1.0 + sc_ref[0]) + sh_ref[0]).astype(BF16)
        h_scr[...] = hb
        lr_ref[...] = _dot(hb, wlr_ref[...])

    o_ref[...] = _dot(h_scr[...], w_ref[...]).astype(o_ref.dtype)


def _in_proj(x2, g, scale, shift, w_main, w_lr, seq):
    t, d = x2.shape
    tm = min(1024, seq)
    tn = 1024
    tiles_per_seq = seq // tm
    return pl.pallas_call(
        _inproj_kernel,
        grid=(t // tm, MAIN_COLS // tn),
        in_specs=[
            pl.BlockSpec((tm, d), lambda i, j: (i, 0)),
            pl.BlockSpec((1, d), lambda i, j: (0, 0)),
            pl.BlockSpec((1, 1, d), lambda i, j: (i // tiles_per_seq, 0, 0)),
            pl.BlockSpec((1, 1, d), lambda i, j: (i // tiles_per_seq, 0, 0)),
            pl.BlockSpec((d, tn), lambda i, j: (0, j)),
            pl.BlockSpec((d, LANES), lambda i, j: (0, 0)),
        ],
        out_specs=[
            pl.BlockSpec((tm, tn), lambda i, j: (i, j)),
            pl.BlockSpec((tm, LANES), lambda i, j: (i, 0)),
        ],
        out_shape=[
            jax.ShapeDtypeStruct((t, MAIN_COLS), BF16),
            jax.ShapeDtypeStruct((t, LANES), F32),
        ],
        scratch_shapes=[pltpu.VMEM((tm, d), BF16)],
        compiler_params=_cparams(("parallel", "arbitrary")),
        name="in_proj",
    )(x2, g.reshape(1, d), scale, shift, w_main, w_lr)


def _gla_kernel(*refs, reverse, bt):
    if reverse:
        q_ref, k_ref, v_ref, lr_ref, wd_ref, bd_ref, o_ref, st_ref = refs
    else:
        (q_ref, k_ref, v_ref, lr_ref, wd_ref, bd_ref, ob_ref, r_ref, gn_ref,
         o_ref, st_ref, o_scr) = refs

    @pl.when(pl.program_id(1) == 0)
    def _():
        st_ref[...] = jnp.zeros_like(st_ref)

    z = _dot3(lr_ref[...], wd_ref[...]) + bd_ref[...]
    la = (jnp.minimum(z, 0.0) - jnp.log1p(jnp.exp(-jnp.abs(z)))) * (1.0 / GATE_NORMALIZER)

    rows = lax.broadcasted_iota(jnp.int32, (bt, bt), 0)
    cols = lax.broadcasted_iota(jnp.int32, (bt, bt), 1)
    same_chunk = (rows // CHUNK) == (cols // CHUNK)
    ordered = (cols >= rows) if reverse else (cols <= rows)
    tri = jnp.where(same_chunk & ordered, 1.0, 0.0).astype(BF16)
    l1, l2, l3 = _split3(la)
    cum = _dot(tri, l1) + (_dot(tri, l2) + _dot(tri, l3))

    crow = lax.broadcasted_iota(jnp.int32, (CHUNK, CHUNK), 0)
    ccol = lax.broadcasted_iota(jnp.int32, (CHUNK, CHUNK), 1)
    att_mask = (ccol > crow) if reverse else (ccol <= crow)

    n_chunks = bt // CHUNK
    chunk_order = range(n_chunks - 1, -1, -1) if reverse else range(n_chunks)
    q_scale = HEAD_K ** -0.5
    for h in range(GLA_HEADS):
        hk = slice(h * HEAD_K, (h + 1) * HEAD_K)
        hv = slice(h * HEAD_V, (h + 1) * HEAD_V)
        state = st_ref[h]
        for c in chunk_order:
            sl = slice(c * CHUNK, (c + 1) * CHUNK)
            cm = cum[sl, hk]
            last = cm[0:1] if reverse else cm[CHUNK - 1:CHUNK]
            qf = q_ref[sl, hk].astype(F32) * q_scale
            kf = k_ref[sl, hk].astype(F32)
            vb = v_ref[sl, hv]
            q_dec = (qf * jnp.exp(cm)).astype(BF16)
            k_inv = (kf * jnp.exp(-cm)).astype(BF16)
            k_end = (kf * jnp.exp(last - cm)).astype(BF16)
            att = lax.dot_general(q_dec, k_inv, (((1,), (1,)), ((), ())), preferred_element_type=F32)
            att = jnp.where(att_mask, att, 0.0).astype(BF16)
            o = _dot(att, vb) + _dot(q_dec, state.astype(BF16))
            if reverse:
                o_ref[sl, hv] = o
            else:
                o_scr[sl, hv] = o
            dec_col = jnp.transpose(jnp.broadcast_to(jnp.exp(last), (HEAD_K, HEAD_K)))
            dec = jnp.concatenate([dec_col, dec_col], axis=1)
            kv = lax.dot_general(k_end, vb, (((0,), (0,)), ((), ())), preferred_element_type=F32)
            state = dec * state + kv
        st_ref[h] = state

    if not reverse:
        for h in range(GLA_HEADS):
            hv = slice(h * HEAD_V, (h + 1) * HEAD_V)
            o = o_scr[:, hv] + ob_ref[:, hv]
            ms = jnp.mean(o * o, axis=-1, keepdims=True)
            y = o * lax.rsqrt(ms + EPS) * gn_ref[...]
            r = r_ref[:, hv].astype(F32)
            o_ref[:, hv] = (y * (r * jax.nn.sigmoid(r))).astype(o_ref.dtype)


def _gla_scan(main, lr, wdec, bdec, seq, reverse, o_bwd=None, g_norm=None):
    t = main.shape[0]
    bsz = t // seq
    bt = min(256, seq)
    nb = seq // bt

    def row(b, n):
        return b * nb + ((nb - 1 - n) if reverse else n)

    in_specs = [
        pl.BlockSpec((bt, GLA_KDIM), lambda b, n: (row(b, n), 0)),
        pl.BlockSpec((bt, GLA_KDIM), lambda b, n: (row(b, n), 1)),
        pl.BlockSpec((bt, GLA_VDIM), lambda b, n: (row(b, n), 1)),
        pl.BlockSpec((bt, LANES), lambda b, n: (row(b, n), 0)),
        pl.BlockSpec((LANES, GLA_KDIM), lambda b, n: (0, 0)),
        pl.BlockSpec((1, GLA_KDIM), lambda b, n: (0, 0)),
    ]
    args = [main, main, main, lr, wdec, bdec]
    scratch = [pltpu.VMEM((GLA_HEADS, HEAD_K, HEAD_V), F32)]
    if reverse:
        out_dtype = F32
    else:
        in_specs += [
            pl.BlockSpec((bt, GLA_VDIM), lambda b, n: (row(b, n), 0)),
            pl.BlockSpec((bt, GLA_VDIM), lambda b, n: (row(b, n), 2)),
            pl.BlockSpec((1, HEAD_V), lambda b, n: (0, 0)),
        ]
        args += [o_bwd, main, g_norm.reshape(1, HEAD_V)]
        scratch.append(pltpu.VMEM((bt, GLA_VDIM), F32))
        out_dtype = BF16
    return pl.pallas_call(
        functools.partial(_gla_kernel, reverse=reverse, bt=bt),
        grid=(bsz, nb),
        in_specs=in_specs,
        out_specs=pl.BlockSpec((bt, GLA_VDIM), lambda b, n: (row(b, n), 0)),
        out_shape=jax.ShapeDtypeStruct((t, GLA_VDIM), out_dtype),
        scratch_shapes=scratch,
        compiler_params=_cparams(("parallel", "arbitrary")),
        name="gla_bwd" if reverse else "gla_fwd",
    )(*args)


def _pool_kernel(up_ref, uc_ref, un_ref, w_ref, b_ref, s_ref, o_ref, ext_ref, *, tm, tiles_per_seq, seq):
    it = pl.program_id(0) % tiles_per_seq
    prev = up_ref[...].astype(F32)
    nxt = un_ref[...].astype(F32)
    ext_ref[0:POOL_HALO, :] = jnp.where(it == 0, 0.0, prev)
    ext_ref[POOL_HALO:POOL_HALO + tm, :] = uc_ref[...].astype(F32)
    ext_ref[POOL_HALO + tm:, :] = jnp.where(it == tiles_per_seq - 1, 0.0, nxt)
    pos = it * tm + lax.broadcasted_iota(jnp.int32, (tm, POOL_GROUP_DIM), 0)
    for gi, w in enumerate(POOL_WINDOWS):
        cs = slice(gi * POOL_GROUP_DIM, (gi + 1) * POOL_GROUP_DIM)
        start = POOL_HALO - w // 2
        acc = ext_ref[start:start + tm, cs]
        for j in range(1, w):
            acc = acc + ext_ref[start + j:start + j + tm, cs]
        cnt = jnp.minimum(pos + w // 2, seq) - jnp.maximum(pos - w // 2, 0)
        p = acc / cnt.astype(F32) - ext_ref[POOL_HALO:POOL_HALO + tm, cs]
        y = _dot(p.astype(BF16), w_ref[gi])
        o_ref[:, cs] = ((y + b_ref[:, cs]) * s_ref[:, cs]).astype(o_ref.dtype)


def _pool_mixer(main, w, b, scale, seq):
    t = main.shape[0]
    tm = min(512, seq)
    tiles_per_seq = seq // tm
    hb = tm // POOL_HALO
    n_hblocks = t // POOL_HALO
    u_col = 3
    return pl.pallas_call(
        functools.partial(_pool_kernel, tm=tm, tiles_per_seq=tiles_per_seq, seq=seq),
        grid=(t // tm,),
        in_specs=[
            pl.BlockSpec((POOL_HALO, POOL_DIM), lambda i: (jnp.maximum(i * hb - 1, 0), u_col)),
            pl.BlockSpec((tm, POOL_DIM), lambda i: (i, u_col)),
            pl.BlockSpec((POOL_HALO, POOL_DIM), lambda i: (jnp.minimum((i + 1) * hb, n_hblocks - 1), u_col)),
            pl.BlockSpec((len(POOL_WINDOWS), POOL_GROUP_DIM, POOL_GROUP_DIM), lambda i: (0, 0, 0)),
            pl.BlockSpec((1, POOL_DIM), lambda i: (0, 0)),
            pl.BlockSpec((1, POOL_DIM), lambda i: (0, 0)),
        ],
        out_specs=pl.BlockSpec((tm, POOL_DIM), lambda i: (i, 0)),
        out_shape=jax.ShapeDtypeStruct((t, POOL_DIM), BF16),
        scratch_shapes=[pltpu.VMEM((tm + 2 * POOL_HALO, POOL_DIM), F32)],
        compiler_params=_cparams(("parallel",)),
        name="pool_mixer",
    )(main, main, main, w, b.reshape(1, POOL_DIM), scale.reshape(1, POOL_DIM))


def _merge_kernel(a_ref, b_ref, ga_ref, gb_ref, x_ref, wa_ref, wb_ref, wo_ref, g1_ref,
                  fg_ref, sc_ref, sh_ref, wrh_ref, wrl_ref, br_ref, xo_ref, h_ref, rt_ref):
    ya = _dot(a_ref[...], wa_ref[...])
    yb = _dot(b_ref[...], wb_ref[...])
    merged = jax.nn.sigmoid(ga_ref[...].astype(F32)) * ya + jax.nn.sigmoid(gb_ref[...].astype(F32)) * yb
    xn = x_ref[...] + g1_ref[0] * _dot(merged.astype(BF16), wo_ref[...])
    xo_ref[...] = xn
    ms = jnp.mean(xn * xn, axis=-1, keepdims=True)
    h = xn * lax.rsqrt(ms + EPS) * fg_ref[...] * (1.0 + sc_ref[0]) + sh_ref[0]
    h_ref[...] = h

    hh, hl = _split2(h)
    lg = _dot(hh, wrh_ref[...]) + (_dot(hh, wrl_ref[...]) + _dot(hl, wrh_ref[...])) + br_ref[...]
    lane = lax.broadcasted_iota(jnp.int32, lg.shape, 1).astype(F32)
    neg = -jnp.inf
    far = float(LANES)
    coarse = lane < N_GROUPS
    cmax = jnp.max(jnp.where(coarse, lg, neg), axis=-1, keepdims=True)
    g_star = jnp.min(jnp.where(coarse & (lg == cmax), lane, far), axis=-1, keepdims=True)
    p_g = 1.0 / jnp.sum(jnp.where(coarse, jnp.exp(lg - cmax), 0.0), axis=-1, keepdims=True)
    lo = N_GROUPS + EXPERTS_PER_GROUP * g_star
    fine = (lane >= lo) & (lane < lo + EXPERTS_PER_GROUP)
    v1 = jnp.max(jnp.where(fine, lg, neg), axis=-1, keepdims=True)
    i1 = jnp.min(jnp.where(fine & (lg == v1), lane, far), axis=-1, keepdims=True)
    fine2 = fine & (lane != i1)
    v2 = jnp.max(jnp.where(fine2, lg, neg), axis=-1, keepdims=True)
    i2 = jnp.min(jnp.where(fine2 & (lg == v2), lane, far), axis=-1, keepdims=True)
    e = jnp.exp(v2 - v1)
    w1 = p_g / (1.0 + e)
    w2 = p_g * e / (1.0 + e)
    rt = jnp.where(lane == 0, i1 - N_GROUPS,
                   jnp.where(lane == 1, i2 - N_GROUPS,
                             jnp.where(lane == 2, w1, jnp.where(lane == 3, w2, 0.0))))
    rt_ref[...] = rt


def _merge(a_in, b_in, main, x2, wa, wb, wo, gate1, fg, scale2, shift2, wr_hi, wr_lo, br, seq):
    t, d = x2.shape
    tm = min(256, seq)
    tiles_per_seq = seq // tm
    const = dict(pipeline_mode=pl.Buffered(1))
    bidx = lambda i: (i // tiles_per_seq, 0, 0)
    return pl.pallas_call(
        _merge_kernel,
        grid=(t // tm,),
        in_specs=[
            pl.BlockSpec((tm, GLA_VDIM), lambda i: (i, 0)),
            pl.BlockSpec((tm, POOL_DIM), lambda i: (i, 0)),
            pl.BlockSpec((tm, d), lambda i: (i, 2)),
            pl.BlockSpec((tm, d), lambda i: (i, 3)),
            pl.BlockSpec((tm, d), lambda i: (i, 0)),
            pl.BlockSpec((GLA_VDIM, d), lambda i: (0, 0), **const),
            pl.BlockSpec((POOL_DIM, d), lambda i: (0, 0), **const),
            pl.BlockSpec((d, d), lambda i: (0, 0), **const),
            pl.BlockSpec((1, 1, d), bidx),
            pl.BlockSpec((1, d), lambda i: (0, 0)),
            pl.BlockSpec((1, 1, d), bidx),
            pl.BlockSpec((1, 1, d), bidx),
            pl.BlockSpec((d, LANES), lambda i: (0, 0), **const),
            pl.BlockSpec((d, LANES), lambda i: (0, 0), **const),
            pl.BlockSpec((1, LANES), lambda i: (0, 0)),
        ],
        out_specs=[
            pl.BlockSpec((tm, d), lambda i: (i, 0)),
            pl.BlockSpec((tm, d), lambda i: (i, 0)),
            pl.BlockSpec((tm, LANES), lambda i: (i, 0)),
        ],
        out_shape=[
            jax.ShapeDtypeStruct((t, d), F32),
            jax.ShapeDtypeStruct((t, d), F32),
            jax.ShapeDtypeStruct((t, LANES), F32),
        ],
        compiler_params=_cparams(("parallel",)),
        name="merge",
    )(a_in, b_in, main, main, x2, wa, wb, wo, gate1, fg.reshape(1, d), scale2, shift2, wr_hi, wr_lo, br)


def _moe_kernel(be_ref, nv_ref, j0_ref, src_ref, dst_ref, h_hbm, w1_ref, w3_ref, w2_ref, y_hbm,
                xbuf, ybuf, w1b, w3b, w2b, wview, gsem, ssem, *, n_blocks):
    b = pl.program_id(0)
    slot = b % 2

    def gather_copy(sl, src_row, g, u):
        return pltpu.make_async_copy(h_hbm.at[pl.ds(src_row, 1)], xbuf.at[sl, g, pl.ds(u, 1)], gsem.at[sl])

    def scatter_copy(sl, g, u, dst_row):
        return pltpu.make_async_copy(ybuf.at[sl, g, pl.ds(u, 1)], y_hbm.at[pl.ds(dst_row, 1)], ssem.at[sl])

    def gather_rows(blk):
        return (nv_ref[blk] + ISSUE_UNROLL - 1) // ISSUE_UNROLL * ISSUE_UNROLL

    def issue_gather(blk, sl):
        base = j0_ref[blk]

        def group(g, carry):
            for u in range(ISSUE_UNROLL):
                gather_copy(sl, src_ref[base + g * ISSUE_UNROLL + u], g, u).start()
            return carry
        lax.fori_loop(0, gather_rows(blk) // ISSUE_UNROLL, group, 0)

    def issue_scatter(blk, sl):
        base = j0_ref[blk]
        n = nv_ref[blk]
        full = n // ISSUE_UNROLL

        def group(g, carry):
            for u in range(ISSUE_UNROLL):
                scatter_copy(sl, g, u, dst_ref[base + g * ISSUE_UNROLL + u]).start()
            return carry
        lax.fori_loop(0, full, group, 0)

        def tail(i, carry):
            scatter_copy(sl, full, i - full * ISSUE_UNROLL, dst_ref[base + i]).start()
            return carry
        lax.fori_loop(full * ISSUE_UNROLL, n, tail, 0)

    def wait_rows(sem, n):
        pltpu.make_async_copy(wview.at[pl.ds(0, n)], wview.at[pl.ds(0, n)], sem).wait()

    def wait_gather(blk, sl):
        wait_rows(gsem.at[sl], gather_rows(blk))

    def wait_scatter(blk, sl):
        wait_rows(ssem.at[sl], nv_ref[blk])

    @pl.when(b == 0)
    def _():
        xbuf[...] = jnp.zeros_like(xbuf)
        issue_gather(0, 0)

    @pl.when(b + 1 < n_blocks)
    def _():
        issue_gather(b + 1, 1 - slot)

    @pl.when((b >= 2) & (nv_ref[jnp.maximum(b - 2, 0)] > 0))
    def _():
        wait_scatter(b - 2, slot)

    @pl.when(nv_ref[b] > 0)
    def _():
        prev_e = be_ref[jnp.maximum(b - 1, 0)]

        @pl.when((b == 0) | (be_ref[b] != prev_e))
        def _():
            w1b[...] = w1_ref[0, 0].astype(BF16)
            w3b[...] = w3_ref[0, 0].astype(BF16)
            w2b[...] = w2_ref[0, 0].astype(BF16)

        wait_gather(b, slot)
        xb = xbuf[slot].reshape(MOE_BLOCK, xbuf.shape[-1]).astype(BF16)
        a = _dot(xb, w1b[...])
        g = _dot(xb, w3b[...])
        mid = (a * jax.nn.sigmoid(a) * g).astype(BF16)
        ybuf[slot] = _dot(mid, w2b[...]).reshape(ybuf.shape[1:])
        issue_scatter(b, slot)

    @pl.when(b == n_blocks - 1)
    def _():
        @pl.when((b >= 1) & (nv_ref[jnp.maximum(b - 1, 0)] > 0))
        def _():
            wait_scatter(b - 1, 1 - slot)

        @pl.when(nv_ref[b] > 0)
        def _():
            wait_scatter(b, slot)


def _moe_experts(h2, block_expert, n_valid, j0, src_rows, dst_rows, w1, w3, w2, layer):
    t, d = h2.shape
    n_blocks = block_expert.shape[0]
    f = w1.shape[-1]
    grid_spec = pltpu.PrefetchScalarGridSpec(
        num_scalar_prefetch=5,
        grid=(n_blocks,),
        in_specs=[
            pl.BlockSpec(memory_space=pl.ANY),
            pl.BlockSpec((1, 1, d, f), lambda b, be, nv, j0, sr, ds: (layer, be[b], 0, 0)),
            pl.BlockSpec((1, 1, d, f), lambda b, be, nv, j0, sr, ds: (layer, be[b], 0, 0)),
            pl.BlockSpec((1, 1, f, d), lambda b, be, nv, j0, sr, ds: (layer, be[b], 0, 0)),
        ],
        out_specs=pl.BlockSpec(memory_space=pl.ANY),
        scratch_shapes=[
            pltpu.VMEM((2, MOE_BLOCK // ISSUE_UNROLL, ISSUE_UNROLL, d), F32),
            pltpu.VMEM((2, MOE_BLOCK // ISSUE_UNROLL, ISSUE_UNROLL, d), F32),
            pltpu.VMEM((d, f), BF16),
            pltpu.VMEM((d, f), BF16),
            pltpu.VMEM((f, d), BF16),
            pltpu.VMEM((MOE_BLOCK, d // LANES, LANES), F32),
            pltpu.SemaphoreType.DMA((2,)),
            pltpu.SemaphoreType.DMA((2,)),
        ],
    )
    return pl.pallas_call(
        functools.partial(_moe_kernel, n_blocks=n_blocks),
        grid_spec=grid_spec,
        out_shape=jax.ShapeDtypeStruct((TOP_K_FINE * t, d), F32),
        compiler_params=_cparams(("arbitrary",)),
        name="moe_experts",
    )(block_expert, n_valid, j0, src_rows, dst_rows, h2, w1, w3, w2)


def _route_plan(route, t):
    n_assign = t * TOP_K_FINE
    n_blocks = -(-n_assign // MOE_BLOCK) + N_EXPERTS
    e_flat = route[:, :TOP_K_FINE].astype(jnp.int32).reshape(-1)
    eids = jnp.arange(N_EXPERTS, dtype=jnp.int32)
    counts = jnp.sum((e_flat[:, None] == eids[None, :]).astype(jnp.int32), axis=0)
    start = jnp.cumsum(counts) - counts
    nblk = (counts + MOE_BLOCK - 1) // MOE_BLOCK
    bend = jnp.cumsum(nblk)
    bstart = bend - nblk
    order = jnp.argsort(e_flat, stable=True).astype(jnp.int32)
    blocks = jnp.arange(n_blocks, dtype=jnp.int32)[:, None]
    owner = ((blocks >= bstart[None, :]) & (blocks < bend[None, :])).astype(jnp.int32)
    block_expert = jnp.minimum(jnp.sum((blocks >= bend[None, :]).astype(jnp.int32), axis=1), N_EXPERTS - 1)
    in_expert = blocks[:, 0] - jnp.sum(owner * bstart[None, :], axis=1)
    n_valid = jnp.clip(jnp.sum(owner * counts[None, :], axis=1) - in_expert * MOE_BLOCK, 0, MOE_BLOCK)
    j0 = jnp.sum(owner * start[None, :], axis=1) + in_expert * MOE_BLOCK
    pad = jnp.zeros((MOE_BLOCK,), jnp.int32)
    src_rows = jnp.concatenate([order // TOP_K_FINE, pad])
    dst_rows = jnp.concatenate([(order % TOP_K_FINE) * t + order // TOP_K_FINE, pad])
    return block_expert, n_valid, j0, src_rows, dst_rows


def _combine_kernel(x_ref, y0_ref, y1_ref, rt_ref, g2_ref, fg_ref, o_ref, *, final):
    rt = rt_ref[...]
    w0 = rt[:, 2:3]
    w1 = rt[:, 3:4]
    xn = x_ref[...] + g2_ref[0] * (w0 * y0_ref[...] + w1 * y1_ref[...])
    if final:
        ms = jnp.mean(xn * xn, axis=-1, keepdims=True)
        xn = xn * lax.rsqrt(ms + EPS) * fg_ref[...]
    o_ref[...] = xn


def _combine(x2, y, route, gate2, final_g, seq, final):
    t, d = x2.shape
    tm = min(256, seq)
    tiles_per_seq = seq // tm
    n_tiles = t // tm
    return pl.pallas_call(
        functools.partial(_combine_kernel, final=final),
        grid=(n_tiles,),
        in_specs=[
            pl.BlockSpec((tm, d), lambda i: (i, 0)),
            pl.BlockSpec((tm, d), lambda i: (i, 0)),
            pl.BlockSpec((tm, d), lambda i: (i + n_tiles, 0)),
            pl.BlockSpec((tm, LANES), lambda i: (i, 0)),
            pl.BlockSpec((1, 1, d), lambda i: (i // tiles_per_seq, 0, 0)),
            pl.BlockSpec((1, d), lambda i: (0, 0)),
        ],
        out_specs=pl.BlockSpec((tm, d), lambda i: (i, 0)),
        out_shape=jax.ShapeDtypeStruct((t, d), F32),
        compiler_params=_cparams(("parallel",)),
        name="combine_final" if final else "combine",
    )(x2, y, y, route, gate2, final_g.reshape(1, d))


def _reorder_in_w(in_w_l):
    lr0 = 2 * GLA_KDIM + 2 * GLA_VDIM
    lr1 = lr0 + 2 * DECAY_RANK
    w_main = jnp.concatenate([in_w_l[:, :lr0], in_w_l[:, lr1:]], axis=1).astype(BF16)
    w_lr = jnp.zeros((in_w_l.shape[0], LANES), BF16).at[:, :2 * DECAY_RANK].set(in_w_l[:, lr0:lr1].astype(BF16))
    return w_main, w_lr


def kernel(x, c, ada_w, ada_b, mix_norm_g, in_w, decay_fw_w, decay_fw_b, decay_bw_w, decay_bw_b, gla_norm_g, pool_w, pool_b, pool_scale, branch_a_w, branch_b_w, out_w, ffn_norm_g, router_coarse_w, router_coarse_b, router_fine_w, router_fine_b, expert_w1, expert_w3, expert_w2, final_norm_g):
    bsz, seq, d = x.shape
    t = bsz * seq
    n_layers = ada_w.shape[0]
    mod = _ada_mod(c, ada_w, ada_b)
    x2 = x.reshape(t, d)
    for l in range(n_layers):
        shift1, scale1, gate1, shift2, scale2, gate2 = [mod[l, :, i] for i in range(N_MOD)]
        w_main, w_lr = _reorder_in_w(in_w[l])
        main, lr = _in_proj(x2, mix_norm_g[l], scale1, shift1, w_main, w_lr, seq)

        wdec_f = jnp.zeros((LANES, GLA_KDIM), F32).at[:DECAY_RANK].set(decay_fw_w[l])
        wdec_b = jnp.zeros((LANES, GLA_KDIM), F32).at[DECAY_RANK:2 * DECAY_RANK].set(decay_bw_w[l])
        o_bwd = _gla_scan(main, lr, wdec_b, decay_bw_b[l].reshape(1, GLA_KDIM), seq, reverse=True)
        a_in = _gla_scan(main, lr, wdec_f, decay_fw_b[l].reshape(1, GLA_KDIM), seq, reverse=False,
                         o_bwd=o_bwd, g_norm=gla_norm_g[l])
        b_in = _pool_mixer(main, pool_w[l].astype(BF16), pool_b[l], pool_scale[l], seq)

        wr = jnp.zeros((d, LANES), F32)
        wr = wr.at[:, :N_GROUPS].set(router_coarse_w[l]).at[:, N_GROUPS:N_GROUPS + N_EXPERTS].set(router_fine_w[l])
        wr_hi = wr.astype(BF16)
        wr_lo = (wr - wr_hi.astype(F32)).astype(BF16)
        br = jnp.zeros((1, LANES), F32)
        br = br.at[0, :N_GROUPS].set(router_coarse_b[l]).at[0, N_GROUPS:N_GROUPS + N_EXPERTS].set(router_fine_b[l])
        x2, h2, route = _merge(a_in, b_in, main, x2, branch_a_w[l].astype(BF16), branch_b_w[l].astype(BF16),
                               out_w[l].astype(BF16), gate1, ffn_norm_g[l], scale2, shift2, wr_hi, wr_lo, br, seq)

        block_expert, n_valid, j0, src_rows, dst_rows = _route_plan(route, t)
        y = _moe_experts(h2, block_expert, n_valid, j0, src_rows, dst_rows, expert_w1, expert_w3, expert_w2, l)
        x2 = _combine(x2, y, route, gate2, final_norm_g, seq, final=(l == n_layers - 1))
    return x2.reshape(bsz, seq, d)
```

```python
import functools

import jax
import jax.numpy as jnp
from jax import lax
from jax.experimental import pallas as pl
from jax.experimental.pallas import tpu as pltpu

F32 = jnp.float32
BF16 = jnp.bfloat16

D_MODEL = 2048
GLA_HEADS = 4
GLA_VDIM = 1024
GLA_KDIM = 512
HEAD_K = 128
HEAD_V = 256
DECAY_RANK = 16
GATE_NORMALIZER = 16.0
LOG2_E = 1.4426950408889634
CHUNK = 128
CUMSUM_ROWS = 256
POOL_DIM = 1024
POOL_WINDOWS = (2, 4, 8, 16)
POOL_GROUP_DIM = 256
POOL_HALO = 16
N_GROUPS = 8
EXPERTS_PER_GROUP = 8
N_EXPERTS = 64
TOP_K_FINE = 2
D_FF_EXPERT = 256
MOE_BLOCK = 128
ISSUE_UNROLL = 8
N_MOD = 6
EPS = 1e-6
LANES = 128
MAIN_COLS = 8192
VMEM_LIMIT = 56 * 1024 * 1024


def _cparams(sem):
    return pltpu.CompilerParams(dimension_semantics=sem, vmem_limit_bytes=VMEM_LIMIT)


def _split2(a):
    hi = a.astype(BF16)
    lo = (a - hi.astype(F32)).astype(BF16)
    return hi, lo


def _split3(a):
    p1 = a.astype(BF16)
    r1 = a - p1.astype(F32)
    p2 = r1.astype(BF16)
    p3 = (r1 - p2.astype(F32)).astype(BF16)
    return p1, p2, p3


def _dot(a, b):
    return jnp.dot(a, b, preferred_element_type=F32)


def _dot3(a, b):
    ah, al = _split2(a)
    bh, bl = _split2(b)
    return _dot(ah, bh) + (_dot(ah, bl) + _dot(al, bh))


def _ada_kernel(c_ref, w_ref, b_ref, o_ref):
    c = c_ref[...]
    s = (c * jax.nn.sigmoid(c)).astype(BF16)
    o_ref[0] = _dot(s, w_ref[0].astype(BF16)) + b_ref[0]


def _ada_mod(c, ada_w, ada_b):
    n_layers, d, n = ada_w.shape
    bsz = c.shape[0]
    rows = 8
    tn = 1024
    c_pad = jnp.zeros((rows, d), F32).at[:bsz].set(c)
    out = pl.pallas_call(
        _ada_kernel,
        grid=(n_layers, n // tn),
        in_specs=[
            pl.BlockSpec((rows, d), lambda l, j: (0, 0)),
            pl.BlockSpec((1, d, tn), lambda l, j: (l, 0, j)),
            pl.BlockSpec((1, 1, tn), lambda l, j: (l, 0, j)),
        ],
        out_specs=pl.BlockSpec((1, rows, tn), lambda l, j: (l, 0, j)),
        out_shape=jax.ShapeDtypeStruct((n_layers, rows, n), F32),
        compiler_params=_cparams(("parallel", "parallel")),
        name="ada_mod",
    )(c_pad, ada_w, ada_b.reshape(n_layers, 1, n))
    return out[:, :bsz].reshape(n_layers, bsz, N_MOD, 1, d)


def _inproj_kernel(x_ref, g_ref, sc_ref, sh_ref, w_ref, wlr_ref, o_ref, lr_ref, h_scr):
    @pl.when(pl.program_id(1) == 0)
    def _():
        x = x_ref[...]
        ms = jnp.mean(x * x, axis=-1, keepdims=True)
        y = x * lax.rsqrt(ms + EPS) * g_ref[...]
        hb = (y * (1.0 + sc_ref[0]) + sh_ref[0]).astype(BF16)
        h_scr[...] = hb
        lr_ref[...] = _dot(hb, wlr_ref[...])

    o_ref[...] = _dot(h_scr[...], w_ref[...]).astype(o_ref.dtype)


def _in_proj(x2, g, scale, shift, w_main, w_lr, seq):
    t, d = x2.shape
    tm = min(1024, seq)
    tn = 1024
    tiles_per_seq = seq // tm
    return pl.pallas_call(
        _inproj_kernel,
        grid=(t // tm, MAIN_COLS // tn),
        in_specs=[
            pl.BlockSpec((tm, d), lambda i, j: (i, 0)),
            pl.BlockSpec((1, d), lambda i, j: (0, 0)),
            pl.BlockSpec((1, 1, d), lambda i, j: (i // tiles_per_seq, 0, 0)),
            pl.BlockSpec((1, 1, d), lambda i, j: (i // tiles_per_seq, 0, 0)),
            pl.BlockSpec((d, tn), lambda i, j: (0, j)),
            pl.BlockSpec((d, LANES), lambda i, j: (0, 0)),
        ],
        out_specs=[
            pl.BlockSpec((tm, tn), lambda i, j: (i, j)),
            pl.BlockSpec((tm, LANES), lambda i, j: (i, 0)),
        ],
        out_shape=[
            jax.ShapeDtypeStruct((t, MAIN_COLS), BF16),
            jax.ShapeDtypeStruct((t, LANES), F32),
        ],
        scratch_shapes=[pltpu.VMEM((tm, d), BF16)],
        compiler_params=_cparams(("parallel", "arbitrary")),
        name="in_proj",
    )(x2, g.reshape(1, d), scale, shift, w_main, w_lr)


def _gla_kernel(*refs, reverse, bt):
    if reverse:
        q_ref, k_ref, v_ref, lr_ref, wd_ref, bd_ref, o_ref, st_ref = refs
    else:
        (q_ref, k_ref, v_ref, lr_ref, wd_ref, bd_ref, ob_ref, r_ref, gn_ref,
         o_ref, st_ref, o_scr) = refs

    @pl.when(pl.program_id(1) == 0)
    def _():
        st_ref[...] = jnp.zeros_like(st_ref)

    z = _dot3(lr_ref[...], wd_ref[...]) + bd_ref[...]
    la = (jnp.minimum(z, 0.0) - jnp.log(1.0 + jnp.exp(-jnp.abs(z)))) * (LOG2_E / GATE_NORMALIZER)

    tb = min(bt, CUMSUM_ROWS)
    rows = lax.broadcasted_iota(jnp.int32, (tb, tb), 0)
    cols = lax.broadcasted_iota(jnp.int32, (tb, tb), 1)
    same_chunk = (rows // CHUNK) == (cols // CHUNK)
    ordered = (cols >= rows) if reverse else (cols <= rows)
    tri = jnp.where(same_chunk & ordered, 1.0, 0.0).astype(BF16)
    l1, l2, l3 = _split3(la)
    cum = jnp.concatenate(
        [_dot(tri, l1[s:s + tb]) + (_dot(tri, l2[s:s + tb]) + _dot(tri, l3[s:s + tb])) for s in range(0, bt, tb)],
        axis=0)

    crow = lax.broadcasted_iota(jnp.int32, (CHUNK, CHUNK), 0)
    ccol = lax.broadcasted_iota(jnp.int32, (CHUNK, CHUNK), 1)
    att_mask = (ccol > crow) if reverse else (ccol <= crow)

    n_chunks = bt // CHUNK
    chunk_order = range(n_chunks - 1, -1, -1) if reverse else range(n_chunks)
    q_scale = HEAD_K ** -0.5
    for c in chunk_order:
        sl = slice(c * CHUNK, (c + 1) * CHUNK)
        for h in range(GLA_HEADS):
            hk = slice(h * HEAD_K, (h + 1) * HEAD_K)
            hv = slice(h * HEAD_V, (h + 1) * HEAD_V)
            state = st_ref[h]
            cm = cum[sl, hk]
            last = cm[0:1] if reverse else cm[CHUNK - 1:CHUNK]
            mid = cm[CHUNK // 2:CHUNK // 2 + 1]
            qf = q_ref[sl, hk].astype(F32) * q_scale
            kf = k_ref[sl, hk].astype(F32)
            vb = v_ref[sl, hv]
            q_in = (qf * jnp.exp2(cm)).astype(BF16)
            q_att = (qf * jnp.exp2(cm - mid)).astype(BF16)
            k_att = (kf * jnp.exp2(mid - cm)).astype(BF16)
            k_end = (kf * jnp.exp2(last - cm)).astype(BF16)
            att = lax.dot_general(q_att, k_att, (((1,), (1,)), ((), ())), preferred_element_type=F32)
            att = jnp.where(att_mask, att, 0.0).astype(BF16)
            o = _dot(att, vb) + _dot(q_in, state.astype(BF16))
            if reverse:
                o_ref[sl, hv] = o
            else:
                o_scr[sl, hv] = o
            dec_col = jnp.transpose(jnp.broadcast_to(jnp.exp2(last), (HEAD_K, HEAD_K)))
            dec = jnp.concatenate([dec_col, dec_col], axis=1)
            kv = lax.dot_general(k_end, vb, (((0,), (0,)), ((), ())), preferred_element_type=F32)
            st_ref[h] = dec * state + kv

    if not reverse:
        for h in range(GLA_HEADS):
            hv = slice(h * HEAD_V, (h + 1) * HEAD_V)
            o = o_scr[:, hv] + ob_ref[:, hv]
            ms = jnp.mean(o * o, axis=-1, keepdims=True)
            y = o * lax.rsqrt(ms + EPS) * gn_ref[...]
            r = r_ref[:, hv].astype(F32)
            o_ref[:, hv] = (y * (r * jax.nn.sigmoid(r))).astype(o_ref.dtype)


def _gla_scan(main, lr, wdec, bdec, seq, reverse, o_bwd=None, g_norm=None):
    t = main.shape[0]
    bsz = t // seq
    bt = min(512, seq)
    nb = seq // bt

    def row(b, n):
        return b * nb + ((nb - 1 - n) if reverse else n)

    in_specs = [
        pl.BlockSpec((bt, GLA_KDIM), lambda b, n: (row(b, n), 0)),
        pl.BlockSpec((bt, GLA_KDIM), lambda b, n: (row(b, n), 1)),
        pl.BlockSpec((bt, GLA_VDIM), lambda b, n: (row(b, n), 1)),
        pl.BlockSpec((bt, LANES), lambda b, n: (row(b, n), 0)),
        pl.BlockSpec((LANES, GLA_KDIM), lambda b, n: (0, 0)),
        pl.BlockSpec((1, GLA_KDIM), lambda b, n: (0, 0)),
    ]
    args = [main, main, main, lr, wdec, bdec]
    scratch = [pltpu.VMEM((GLA_HEADS, HEAD_K, HEAD_V), F32)]
    if reverse:
        out_dtype = F32
    else:
        in_specs += [
            pl.BlockSpec((bt, GLA_VDIM), lambda b, n: (row(b, n), 0)),
            pl.BlockSpec((bt, GLA_VDIM), lambda b, n: (row(b, n), 2)),
            pl.BlockSpec((1, HEAD_V), lambda b, n: (0, 0)),
        ]
        args += [o_bwd, main, g_norm.reshape(1, HEAD_V)]
        scratch.append(pltpu.VMEM((bt, GLA_VDIM), F32))
        out_dtype = BF16
    return pl.pallas_call(
        functools.partial(_gla_kernel, reverse=reverse, bt=bt),
        grid=(bsz, nb),
        in_specs=in_specs,
        out_specs=pl.BlockSpec((bt, GLA_VDIM), lambda b, n: (row(b, n), 0)),
        out_shape=jax.ShapeDtypeStruct((t, GLA_VDIM), out_dtype),
        scratch_shapes=scratch,
        compiler_params=_cparams(("parallel", "arbitrary")),
        name="gla_bwd" if reverse else "gla_fwd",
    )(*args)


def _pool_kernel(up_ref, uc_ref, un_ref, w_ref, b_ref, s_ref, o_ref, ext_ref, *, tm, tiles_per_seq, seq):
    it = pl.program_id(0) % tiles_per_seq
    prev = up_ref[...].astype(F32)
    nxt = un_ref[...].astype(F32)
    ext_ref[0:POOL_HALO, :] = jnp.where(it == 0, 0.0, prev)
    ext_ref[POOL_HALO:POOL_HALO + tm, :] = uc_ref[...].astype(F32)
    ext_ref[POOL_HALO + tm:, :] = jnp.where(it == tiles_per_seq - 1, 0.0, nxt)
    pos = it * tm + lax.broadcasted_iota(jnp.int32, (tm, POOL_GROUP_DIM), 0)
    for gi, w in enumerate(POOL_WINDOWS):
        cs = slice(gi * POOL_GROUP_DIM, (gi + 1) * POOL_GROUP_DIM)
        start = POOL_HALO - w // 2
        acc = ext_ref[start:start + tm, cs]
        for j in range(1, w):
            acc = acc + ext_ref[start + j:start + j + tm, cs]
        cnt = jnp.minimum(pos + w // 2, seq) - jnp.maximum(pos - w // 2, 0)
        p = acc / cnt.astype(F32) - ext_ref[POOL_HALO:POOL_HALO + tm, cs]
        y = _dot(p.astype(BF16), w_ref[gi])
        o_ref[:, cs] = ((y + b_ref[:, cs]) * s_ref[:, cs]).astype(o_ref.dtype)


def _pool_mixer(main, w, b, scale, seq):
    t = main.shape[0]
    tm = min(512, seq)
    tiles_per_seq = seq // tm
    hb = tm // POOL_HALO
    n_hblocks = t // POOL_HALO
    u_col = 3
    return pl.pallas_call(
        functools.partial(_pool_kernel, tm=tm, tiles_per_seq=tiles_per_seq, seq=seq),
        grid=(t // tm,),
        in_specs=[
            pl.BlockSpec((POOL_HALO, POOL_DIM), lambda i: (jnp.maximum(i * hb - 1, 0), u_col)),
            pl.BlockSpec((tm, POOL_DIM), lambda i: (i, u_col)),
            pl.BlockSpec((POOL_HALO, POOL_DIM), lambda i: (jnp.minimum((i + 1) * hb, n_hblocks - 1), u_col)),
            pl.BlockSpec((len(POOL_WINDOWS), POOL_GROUP_DIM, POOL_GROUP_DIM), lambda i: (0, 0, 0)),
            pl.BlockSpec((1, POOL_DIM), lambda i: (0, 0)),
            pl.BlockSpec((1, POOL_DIM), lambda i: (0, 0)),
        ],
        out_specs=pl.BlockSpec((tm, POOL_DIM), lambda i: (i, 0)),
        out_shape=jax.ShapeDtypeStruct((t, POOL_DIM), BF16),
        scratch_shapes=[pltpu.VMEM((tm + 2 * POOL_HALO, POOL_DIM), F32)],
        compiler_params=_cparams(("parallel",)),
        name="pool_mixer",
    )(main, main, main, w, b.reshape(1, POOL_DIM), scale.reshape(1, POOL_DIM))


def _merge_kernel(a_ref, b_ref, ga_ref, gb_ref, x_ref, wa_ref, wb_ref, wo_ref, g1_ref,
                  fg_ref, sc_ref, sh_ref, wrh_ref, wrl_ref, br_ref, xo_ref, h_ref, rt_ref):
    ya = _dot(a_ref[...], wa_ref[...])
    yb = _dot(b_ref[...], wb_ref[...])
    merged = jax.nn.sigmoid(ga_ref[...].astype(F32)) * ya + jax.nn.sigmoid(gb_ref[...].astype(F32)) * yb
    xn = x_ref[...] + g1_ref[0] * _dot(merged.astype(BF16), wo_ref[...])
    xo_ref[...] = xn
    ms = jnp.mean(xn * xn, axis=-1, keepdims=True)
    h = xn * lax.rsqrt(ms + EPS) * fg_ref[...] * (1.0 + sc_ref[0]) + sh_ref[0]
    h_ref[...] = h

    hh, hl = _split2(h)
    lg = _dot(hh, wrh_ref[...]) + (_dot(hh, wrl_ref[...]) + _dot(hl, wrh_ref[...])) + br_ref[...]
    lane = lax.broadcasted_iota(jnp.int32, lg.shape, 1).astype(F32)
    neg = -jnp.inf
    far = float(LANES)
    coarse = lane < N_GROUPS
    cmax = jnp.max(jnp.where(coarse, lg, neg), axis=-1, keepdims=True)
    g_star = jnp.min(jnp.where(coarse & (lg == cmax), lane, far), axis=-1, keepdims=True)
    p_g = 1.0 / jnp.sum(jnp.where(coarse, jnp.exp(lg - cmax), 0.0), axis=-1, keepdims=True)
    lo = N_GROUPS + EXPERTS_PER_GROUP * g_star
    fine = (lane >= lo) & (lane < lo + EXPERTS_PER_GROUP)
    v1 = jnp.max(jnp.where(fine, lg, neg), axis=-1, keepdims=True)
    i1 = jnp.min(jnp.where(fine & (lg == v1), lane, far), axis=-1, keepdims=True)
    fine2 = fine & (lane != i1)
    v2 = jnp.max(jnp.where(fine2, lg, neg), axis=-1, keepdims=True)
    i2 = jnp.min(jnp.where(fine2 & (lg == v2), lane, far), axis=-1, keepdims=True)
    e = jnp.exp(v2 - v1)
    w1 = p_g / (1.0 + e)
    w2 = p_g * e / (1.0 + e)
    rt = jnp.where(lane == 0, i1 - N_GROUPS,
                   jnp.where(lane == 1, i2 - N_GROUPS,
                             jnp.where(lane == 2, w1, jnp.where(lane == 3, w2, 0.0))))
    rt_ref[...] = rt


def _merge(a_in, b_in, main, x2, wa, wb, wo, gate1, fg, scale2, shift2, wr_hi, wr_lo, br, seq):
    t, d = x2.shape
    tm = min(256, seq)
    tiles_per_seq = seq // tm
    const = dict(pipeline_mode=pl.Buffered(1))
    bidx = lambda i: (i // tiles_per_seq, 0, 0)
    return pl.pallas_call(
        _merge_kernel,
        grid=(t // tm,),
        in_specs=[
            pl.BlockSpec((tm, GLA_VDIM), lambda i: (i, 0)),
            pl.BlockSpec((tm, POOL_DIM), lambda i: (i, 0)),
            pl.BlockSpec((tm, d), lambda i: (i, 2)),
            pl.BlockSpec((tm, d), lambda i: (i, 3)),
            pl.BlockSpec((tm, d), lambda i: (i, 0)),
            pl.BlockSpec((GLA_VDIM, d), lambda i: (0, 0), **const),
            pl.BlockSpec((POOL_DIM, d), lambda i: (0, 0), **const),
            pl.BlockSpec((d, d), lambda i: (0, 0), **const),
            pl.BlockSpec((1, 1, d), bidx),
            pl.BlockSpec((1, d), lambda i: (0, 0)),
            pl.BlockSpec((1, 1, d), bidx),
            pl.BlockSpec((1, 1, d), bidx),
            pl.BlockSpec((d, LANES), lambda i: (0, 0), **const),
            pl.BlockSpec((d, LANES), lambda i: (0, 0), **const),
            pl.BlockSpec((1, LANES), lambda i: (0, 0)),
        ],
        out_specs=[
            pl.BlockSpec((tm, d), lambda i: (i, 0)),
            pl.BlockSpec((tm, d), lambda i: (i, 0)),
            pl.BlockSpec((tm, LANES), lambda i: (i, 0)),
        ],
        out_shape=[
            jax.ShapeDtypeStruct((t, d), F32),
            jax.ShapeDtypeStruct((t, d), F32),
            jax.ShapeDtypeStruct((t, LANES), F32),
        ],
        compiler_params=_cparams(("parallel",)),
        name="merge",
    )(a_in, b_in, main, main, x2, wa, wb, wo, gate1, fg.reshape(1, d), scale2, shift2, wr_hi, wr_lo, br)


def _moe_kernel(be_ref, nv_ref, j0_ref, src_ref, dst_ref, h_hbm, w1_ref, w3_ref, w2_ref, y_hbm,
                xbuf, ybuf, w1b, w3b, w2b, wview, gsem, ssem, *, n_blocks):
    b = pl.program_id(0)
    slot = b % 2

    def gather_copy(sl, src_row, g, u):
        return pltpu.make_async_copy(h_hbm.at[pl.ds(src_row, 1)], xbuf.at[sl, g, pl.ds(u, 1)], gsem.at[sl])

    def scatter_copy(sl, g, u, dst_row):
        return pltpu.make_async_copy(ybuf.at[sl, g, pl.ds(u, 1)], y_hbm.at[pl.ds(dst_row, 1)], ssem.at[sl])

    def issue_gather_group(base, sl, g):
        for u in range(ISSUE_UNROLL):
            gather_copy(sl, src_ref[base + g * ISSUE_UNROLL + u], g, u).start()

    def issue_scatter(blk, sl):
        base = j0_ref[blk]
        n = nv_ref[blk]
        full = n // ISSUE_UNROLL

        def group(g, carry):
            for u in range(ISSUE_UNROLL):
                scatter_copy(sl, g, u, dst_ref[base + g * ISSUE_UNROLL + u]).start()
            return carry
        lax.fori_loop(0, full, group, 0)

        def tail(i, carry):
            scatter_copy(sl, full, i - full * ISSUE_UNROLL, dst_ref[base + i]).start()
            return carry
        lax.fori_loop(full * ISSUE_UNROLL, n, tail, 0)

    def wait_rows(sem, n):
        pltpu.make_async_copy(wview.at[pl.ds(0, n)], wview.at[pl.ds(0, n)], sem).wait()

    def wait_gather(sl):
        wait_rows(gsem.at[sl], MOE_BLOCK)

    def wait_scatter(blk, sl):
        wait_rows(ssem.at[sl], nv_ref[blk])

    n_groups = MOE_BLOCK // ISSUE_UNROLL
    used = nv_ref[b] > 0
    prev_used = nv_ref[jnp.maximum(b - 1, 0)] > 0

    @pl.when(b == 0)
    def _():
        def first(g, carry):
            issue_gather_group(j0_ref[0], 0, g)
            return carry
        lax.fori_loop(0, n_groups, first, 0)

    @pl.when((b >= 2) & (nv_ref[jnp.maximum(b - 2, 0)] > 0))
    def _():
        wait_scatter(b - 2, slot)

    @pl.when(used)
    def _():
        prev_e = be_ref[jnp.maximum(b - 1, 0)]

        @pl.when((b == 0) | (be_ref[b] != prev_e))
        def _():
            w1b[...] = w1_ref[0, 0].astype(BF16)
            w3b[...] = w3_ref[0, 0].astype(BF16)
            w2b[...] = w2_ref[0, 0].astype(BF16)

        wait_gather(slot)
        next_base = j0_ref[jnp.minimum(b + 1, n_blocks - 1)]
        d = xbuf.shape[-1]
        pieces = n_groups // 2
        kc = d // pieces
        xb = xbuf[slot].reshape(MOE_BLOCK, d).astype(BF16)
        a = g = None
        for k in range(pieces):
            ks = slice(k * kc, (k + 1) * kc)
            pa = _dot(xb[:, ks], w1b[ks, :])
            pg = _dot(xb[:, ks], w3b[ks, :])
            a = pa if a is None else a + pa
            g = pg if g is None else g + pg
            issue_gather_group(next_base, 1 - slot, k)
        mid = (a * jax.nn.sigmoid(a) * g).astype(BF16)
        for n in range(pieces):
            ns = slice(n * kc, (n + 1) * kc)
            ybuf[slot, :, :, ns] = _dot(mid, w2b[:, ns]).reshape(n_groups, ISSUE_UNROLL, kc)
            issue_gather_group(next_base, 1 - slot, pieces + n)
        issue_scatter(b, slot)

    @pl.when(jnp.logical_not(used) & (b > 0) & prev_used)
    def _():
        wait_gather(slot)

    @pl.when(b == n_blocks - 1)
    def _():
        @pl.when(used)
        def _():
            wait_gather(1 - slot)

        @pl.when((b >= 1) & prev_used)
        def _():
            wait_scatter(b - 1, 1 - slot)

        @pl.when(used)
        def _():
            wait_scatter(b, slot)


def _moe_experts(h2, block_expert, n_valid, j0, src_rows, dst_rows, w1, w3, w2, layer):
    t, d = h2.shape
    n_blocks = block_expert.shape[0]
    f = w1.shape[-1]
    grid_spec = pltpu.PrefetchScalarGridSpec(
        num_scalar_prefetch=5,
        grid=(n_blocks,),
        in_specs=[
            pl.BlockSpec(memory_space=pl.ANY),
            pl.BlockSpec((1, 1, d, f), lambda b, be, nv, j0, sr, ds: (layer, be[b], 0, 0)),
            pl.BlockSpec((1, 1, d, f), lambda b, be, nv, j0, sr, ds: (layer, be[b], 0, 0)),
            pl.BlockSpec((1, 1, f, d), lambda b, be, nv, j0, sr, ds: (layer, be[b], 0, 0)),
        ],
        out_specs=pl.BlockSpec(memory_space=pl.ANY),
        scratch_shapes=[
            pltpu.VMEM((2, MOE_BLOCK // ISSUE_UNROLL, ISSUE_UNROLL, d), F32),
            pltpu.VMEM((2, MOE_BLOCK // ISSUE_UNROLL, ISSUE_UNROLL, d), F32),
            pltpu.VMEM((d, f), BF16),
            pltpu.VMEM((d, f), BF16),
            pltpu.VMEM((f, d), BF16),
            pltpu.VMEM((MOE_BLOCK, d // LANES, LANES), F32),
            pltpu.SemaphoreType.DMA((2,)),
            pltpu.SemaphoreType.DMA((2,)),
        ],
    )
    return pl.pallas_call(
        functools.partial(_moe_kernel, n_blocks=n_blocks),
        grid_spec=grid_spec,
        out_shape=jax.ShapeDtypeStruct((TOP_K_FINE * t, d), F32),
        compiler_params=_cparams(("arbitrary",)),
        name="moe_experts",
    )(block_expert, n_valid, j0, src_rows, dst_rows, h2, w1, w3, w2)


def _route_plan(route, t):
    n_assign = t * TOP_K_FINE
    n_blocks = -(-n_assign // MOE_BLOCK) + N_EXPERTS
    e_flat = route[:, :TOP_K_FINE].astype(jnp.int32).reshape(-1)
    eids = jnp.arange(N_EXPERTS, dtype=jnp.int32)
    counts = jnp.sum((e_flat[:, None] == eids[None, :]).astype(jnp.int32), axis=0)
    start = jnp.cumsum(counts) - counts
    nblk = (counts + MOE_BLOCK - 1) // MOE_BLOCK
    bend = jnp.cumsum(nblk)
    bstart = bend - nblk
    order = jnp.argsort(e_flat, stable=True).astype(jnp.int32)
    blocks = jnp.arange(n_blocks, dtype=jnp.int32)[:, None]
    owner = ((blocks >= bstart[None, :]) & (blocks < bend[None, :])).astype(jnp.int32)
    block_expert = jnp.minimum(jnp.sum((blocks >= bend[None, :]).astype(jnp.int32), axis=1), N_EXPERTS - 1)
    in_expert = blocks[:, 0] - jnp.sum(owner * bstart[None, :], axis=1)
    n_valid = jnp.clip(jnp.sum(owner * counts[None, :], axis=1) - in_expert * MOE_BLOCK, 0, MOE_BLOCK)
    j0 = jnp.minimum(jnp.sum(owner * start[None, :], axis=1) + in_expert * MOE_BLOCK, n_assign)
    pad = jnp.zeros((MOE_BLOCK,), jnp.int32)
    src_rows = jnp.concatenate([order // TOP_K_FINE, pad])
    dst_rows = jnp.concatenate([(order % TOP_K_FINE) * t + order // TOP_K_FINE, pad])
    return block_expert, n_valid, j0, src_rows, dst_rows


def _combine_kernel(x_ref, y0_ref, y1_ref, rt_ref, g2_ref, fg_ref, o_ref, *, final):
    rt = rt_ref[...]
    w0 = rt[:, 2:3]
    w1 = rt[:, 3:4]
    xn = x_ref[...] + g2_ref[0] * (w0 * y0_ref[...] + w1 * y1_ref[...])
    if final:
        ms = jnp.mean(xn * xn, axis=-1, keepdims=True)
        xn = xn * lax.rsqrt(ms + EPS) * fg_ref[...]
    o_ref[...] = xn


def _combine(x2, y, route, gate2, final_g, seq, final):
    t, d = x2.shape
    tm = min(256, seq)
    tiles_per_seq = seq // tm
    n_tiles = t // tm
    return pl.pallas_call(
        functools.partial(_combine_kernel, final=final),
        grid=(n_tiles,),
        in_specs=[
            pl.BlockSpec((tm, d), lambda i: (i, 0)),
            pl.BlockSpec((tm, d), lambda i: (i, 0)),
            pl.BlockSpec((tm, d), lambda i: (i + n_tiles, 0)),
            pl.BlockSpec((tm, LANES), lambda i: (i, 0)),
            pl.BlockSpec((1, 1, d), lambda i: (i // tiles_per_seq, 0, 0)),
            pl.BlockSpec((1, d), lambda i: (0, 0)),
        ],
        out_specs=pl.BlockSpec((tm, d), lambda i: (i, 0)),
        out_shape=jax.ShapeDtypeStruct((t, d), F32),
        compiler_params=_cparams(("parallel",)),
        name="combine_final" if final else "combine",
    )(x2, y, y, route, gate2, final_g.reshape(1, d))


def _reorder_in_w(in_w_l):
    lr0 = 2 * GLA_KDIM + 2 * GLA_VDIM
    lr1 = lr0 + 2 * DECAY_RANK
    w_main = jnp.concatenate([in_w_l[:, :lr0], in_w_l[:, lr1:]], axis=1).astype(BF16)
    w_lr = jnp.zeros((in_w_l.shape[0], LANES), BF16).at[:, :2 * DECAY_RANK].set(in_w_l[:, lr0:lr1].astype(BF16))
    return w_main, w_lr


def kernel(x, c, ada_w, ada_b, mix_norm_g, in_w, decay_fw_w, decay_fw_b, decay_bw_w, decay_bw_b, gla_norm_g, pool_w, pool_b, pool_scale, branch_a_w, branch_b_w, out_w, ffn_norm_g, router_coarse_w, router_coarse_b, router_fine_w, router_fine_b, expert_w1, expert_w3, expert_w2, final_norm_g):
    bsz, seq, d = x.shape
    t = bsz * seq
    n_layers = ada_w.shape[0]
    mod = _ada_mod(c, ada_w, ada_b)
    x2 = x.reshape(t, d)
    for l in range(n_layers):
        shift1, scale1, gate1, shift2, scale2, gate2 = [mod[l, :, i] for i in range(N_MOD)]
        w_main, w_lr = _reorder_in_w(in_w[l])
        main, lr = _in_proj(x2, mix_norm_g[l], scale1, shift1, w_main, w_lr, seq)

        wdec_f = jnp.zeros((LANES, GLA_KDIM), F32).at[:DECAY_RANK].set(decay_fw_w[l])
        wdec_b = jnp.zeros((LANES, GLA_KDIM), F32).at[DECAY_RANK:2 * DECAY_RANK].set(decay_bw_w[l])
        o_bwd = _gla_scan(main, lr, wdec_b, decay_bw_b[l].reshape(1, GLA_KDIM), seq, reverse=True)
        a_in = _gla_scan(main, lr, wdec_f, decay_fw_b[l].reshape(1, GLA_KDIM), seq, reverse=False,
                         o_bwd=o_bwd, g_norm=gla_norm_g[l])
        b_in = _pool_mixer(main, pool_w[l].astype(BF16), pool_b[l], pool_scale[l], seq)

        wr = jnp.zeros((d, LANES), F32)
        wr = wr.at[:, :N_GROUPS].set(router_coarse_w[l]).at[:, N_GROUPS:N_GROUPS + N_EXPERTS].set(router_fine_w[l])
        wr_hi = wr.astype(BF16)
        wr_lo = (wr - wr_hi.astype(F32)).astype(BF16)
        br = jnp.zeros((1, LANES), F32)
        br = br.at[0, :N_GROUPS].set(router_coarse_b[l]).at[0, N_GROUPS:N_GROUPS + N_EXPERTS].set(router_fine_b[l])
        x2, h2, route = _merge(a_in, b_in, main, x2, branch_a_w[l].astype(BF16), branch_b_w[l].astype(BF16),
                               out_w[l].astype(BF16), gate1, ffn_norm_g[l], scale2, shift2, wr_hi, wr_lo, br, seq)

        block_expert, n_valid, j0, src_rows, dst_rows = _route_plan(route, t)
        y = _moe_experts(h2, block_expert, n_valid, j0, src_rows, dst_rows, expert_w1, expert_w3, expert_w2, l)
        x2 = _combine(x2, y, route, gate2, final_norm_g, seq, final=(l == n_layers - 1))
    return x2.reshape(bsz, seq, d)
```

```python
import functools

import jax
import jax.numpy as jnp
from jax import lax
from jax.experimental import pallas as pl
from jax.experimental.pallas import tpu as pltpu

F32 = jnp.float32
BF16 = jnp.bfloat16

D_MODEL = 2048
GLA_HEADS = 4
GLA_VDIM = 1024
GLA_KDIM = 512
HEAD_K = 128
HEAD_V = 256
DECAY_RANK = 16
GATE_NORMALIZER = 16.0
LOG2_E = 1.4426950408889634
CHUNK = 128
CUMSUM_ROWS = 256
POOL_DIM = 1024
POOL_WINDOWS = (2, 4, 8, 16)
POOL_GROUP_DIM = 256
POOL_HALO = 16
N_GROUPS = 8
EXPERTS_PER_GROUP = 8
N_EXPERTS = 64
TOP_K_FINE = 2
D_FF_EXPERT = 256
MOE_BLOCK = 128
ISSUE_UNROLL = 8
N_MOD = 6
EPS = 1e-6
LANES = 128
MAIN_COLS = 8192
VMEM_LIMIT = 56 * 1024 * 1024


def _cparams(sem):
    return pltpu.CompilerParams(dimension_semantics=sem, vmem_limit_bytes=VMEM_LIMIT)


def _split2(a):
    hi = a.astype(BF16)
    lo = (a - hi.astype(F32)).astype(BF16)
    return hi, lo


def _split3(a):
    p1 = a.astype(BF16)
    r1 = a - p1.astype(F32)
    p2 = r1.astype(BF16)
    p3 = (r1 - p2.astype(F32)).astype(BF16)
    return p1, p2, p3


def _dot(a, b):
    return jnp.dot(a, b, preferred_element_type=F32)


def _dot3(a, b):
    ah, al = _split2(a)
    bh, bl = _split2(b)
    return _dot(ah, bh) + (_dot(ah, bl) + _dot(al, bh))


def _ada_kernel(c_ref, w_ref, b_ref, o_ref):
    c = c_ref[...]
    s = (c * jax.nn.sigmoid(c)).astype(BF16)
    o_ref[0] = _dot(s, w_ref[0].astype(BF16)) + b_ref[0]


def _ada_mod(c, ada_w, ada_b):
    n_layers, d, n = ada_w.shape
    bsz = c.shape[0]
    rows = 8
    tn = 1024
    c_pad = jnp.zeros((rows, d), F32).at[:bsz].set(c)
    out = pl.pallas_call(
        _ada_kernel,
        grid=(n_layers, n // tn),
        in_specs=[
            pl.BlockSpec((rows, d), lambda l, j: (0, 0)),
            pl.BlockSpec((1, d, tn), lambda l, j: (l, 0, j)),
            pl.BlockSpec((1, 1, tn), lambda l, j: (l, 0, j)),
        ],
        out_specs=pl.BlockSpec((1, rows, tn), lambda l, j: (l, 0, j)),
        out_shape=jax.ShapeDtypeStruct((n_layers, rows, n), F32),
        compiler_params=_cparams(("parallel", "parallel")),
        name="ada_mod",
    )(c_pad, ada_w, ada_b.reshape(n_layers, 1, n))
    return out[:, :bsz].reshape(n_layers, bsz, N_MOD, 1, d)


def _moe_residual(x_ref, y0_ref, y1_ref, rt_ref, g2_ref):
    rt = rt_ref[...]
    return x_ref[...] + g2_ref[0] * (rt[:, 2:3] * y0_ref[...] + rt[:, 3:4] * y1_ref[...])


def _inproj_kernel(*refs, fuse_moe):
    if fuse_moe:
        (x_ref, y0_ref, y1_ref, rt_ref, g2_ref, g_ref, sc_ref, sh_ref, w_ref, wlr_ref,
         o_ref, lr_ref, xn_ref, h_scr) = refs
    else:
        x_ref, g_ref, sc_ref, sh_ref, w_ref, wlr_ref, o_ref, lr_ref, h_scr = refs

    @pl.when(pl.program_id(1) == 0)
    def _():
        if fuse_moe:
            x = _moe_residual(x_ref, y0_ref, y1_ref, rt_ref, g2_ref)
            xn_ref[...] = x
        else:
            x = x_ref[...]
        ms = jnp.mean(x * x, axis=-1, keepdims=True)
        y = x * lax.rsqrt(ms + EPS) * g_ref[...]
        hb = (y * (1.0 + sc_ref[0]) + sh_ref[0]).astype(BF16)
        h_scr[...] = hb
        lr_ref[...] = _dot(hb, wlr_ref[0])

    o_ref[...] = _dot(h_scr[...], w_ref[0]).astype(o_ref.dtype)


def _in_proj(x2, g, scale, shift, w_main, w_lr, layer, seq, moe=None):
    t, d = x2.shape
    fuse_moe = moe is not None
    tm = min(512 if fuse_moe else 1024, seq)
    tn = 1024
    tiles_per_seq = seq // tm
    n_tiles = t // tm
    bidx = lambda i, j: (i // tiles_per_seq, 0, 0)
    row = lambda i, j: (i, 0)
    in_specs = [pl.BlockSpec((tm, d), row)]
    args = [x2]
    if fuse_moe:
        y, route, gate2 = moe
        in_specs += [
            pl.BlockSpec((tm, d), row),
            pl.BlockSpec((tm, d), lambda i, j: (i + n_tiles, 0)),
            pl.BlockSpec((tm, LANES), row),
            pl.BlockSpec((1, 1, d), bidx),
        ]
        args += [y, y, route, gate2]
    in_specs += [
        pl.BlockSpec((1, d), lambda i, j: (0, 0)),
        pl.BlockSpec((1, 1, d), bidx),
        pl.BlockSpec((1, 1, d), bidx),
        pl.BlockSpec((1, d, tn), lambda i, j: (layer, 0, j)),
        pl.BlockSpec((1, d, LANES), lambda i, j: (layer, 0, 0)),
    ]
    args += [g.reshape(1, d), scale, shift, w_main, w_lr]
    out_specs = [pl.BlockSpec((tm, tn), lambda i, j: (i, j)), pl.BlockSpec((tm, LANES), row)]
    out_shape = [jax.ShapeDtypeStruct((t, MAIN_COLS), BF16), jax.ShapeDtypeStruct((t, LANES), F32)]
    if fuse_moe:
        out_specs.append(pl.BlockSpec((tm, d), row))
        out_shape.append(jax.ShapeDtypeStruct((t, d), F32))
    return pl.pallas_call(
        functools.partial(_inproj_kernel, fuse_moe=fuse_moe),
        grid=(n_tiles, MAIN_COLS // tn),
        in_specs=in_specs,
        out_specs=out_specs,
        out_shape=out_shape,
        scratch_shapes=[pltpu.VMEM((tm, d), BF16)],
        compiler_params=_cparams(("parallel", "arbitrary")),
        name="in_proj_moe" if fuse_moe else "in_proj",
    )(*args)


def _gla_kernel(*refs, reverse, bt):
    if reverse:
        q_ref, k_ref, v_ref, lr_ref, wd_ref, bd_ref, o_ref, st_ref = refs
    else:
        (q_ref, k_ref, v_ref, lr_ref, wd_ref, bd_ref, ob_ref, r_ref, gn_ref,
         o_ref, st_ref, o_scr) = refs

    @pl.when(pl.program_id(1) == 0)
    def _():
        st_ref[...] = jnp.zeros_like(st_ref)

    z = _dot3(lr_ref[...], wd_ref[...]) + bd_ref[...]
    la = (jnp.minimum(z, 0.0) - jnp.log(1.0 + jnp.exp(-jnp.abs(z)))) * (LOG2_E / GATE_NORMALIZER)

    tb = min(bt, CUMSUM_ROWS)
    rows = lax.broadcasted_iota(jnp.int32, (tb, tb), 0)
    cols = lax.broadcasted_iota(jnp.int32, (tb, tb), 1)
    same_chunk = (rows // CHUNK) == (cols // CHUNK)
    ordered = (cols >= rows) if reverse else (cols <= rows)
    tri = jnp.where(same_chunk & ordered, 1.0, 0.0).astype(BF16)
    l1, l2, l3 = _split3(la)
    cum = jnp.concatenate(
        [_dot(tri, l1[s:s + tb]) + (_dot(tri, l2[s:s + tb]) + _dot(tri, l3[s:s + tb])) for s in range(0, bt, tb)],
        axis=0)

    crow = lax.broadcasted_iota(jnp.int32, (CHUNK, CHUNK), 0)
    ccol = lax.broadcasted_iota(jnp.int32, (CHUNK, CHUNK), 1)
    att_mask = (ccol > crow) if reverse else (ccol <= crow)

    n_chunks = bt // CHUNK
    chunk_order = range(n_chunks - 1, -1, -1) if reverse else range(n_chunks)
    q_scale = HEAD_K ** -0.5
    for c in chunk_order:
        sl = slice(c * CHUNK, (c + 1) * CHUNK)
        for h in range(GLA_HEADS):
            hk = slice(h * HEAD_K, (h + 1) * HEAD_K)
            hv = slice(h * HEAD_V, (h + 1) * HEAD_V)
            state = st_ref[h]
            cm = cum[sl, hk]
            last = cm[0:1] if reverse else cm[CHUNK - 1:CHUNK]
            mid = cm[CHUNK // 2:CHUNK // 2 + 1]
            qf = q_ref[sl, hk].astype(F32) * q_scale
            kf = k_ref[sl, hk].astype(F32)
            vb = v_ref[sl, hv]
            q_in = (qf * jnp.exp2(cm)).astype(BF16)
            q_att = (qf * jnp.exp2(cm - mid)).astype(BF16)
            k_att = (kf * jnp.exp2(mid - cm)).astype(BF16)
            k_end = (kf * jnp.exp2(last - cm)).astype(BF16)
            att = lax.dot_general(q_att, k_att, (((1,), (1,)), ((), ())), preferred_element_type=F32)
            att = jnp.where(att_mask, att, 0.0).astype(BF16)
            o = _dot(att, vb) + _dot(q_in, state.astype(BF16))
            if reverse:
                o_ref[sl, hv] = o
            else:
                o_scr[sl, hv] = o
            dec_col = jnp.transpose(jnp.broadcast_to(jnp.exp2(last), (HEAD_K, HEAD_K)))
            dec = jnp.concatenate([dec_col, dec_col], axis=1)
            kv = lax.dot_general(k_end, vb, (((0,), (0,)), ((), ())), preferred_element_type=F32)
            st_ref[h] = dec * state + kv

    if not reverse:
        for h in range(GLA_HEADS):
            hv = slice(h * HEAD_V, (h + 1) * HEAD_V)
            o = o_scr[:, hv] + ob_ref[:, hv]
            ms = jnp.mean(o * o, axis=-1, keepdims=True)
            y = o * lax.rsqrt(ms + EPS) * gn_ref[...]
            r = r_ref[:, hv].astype(F32)
            o_ref[:, hv] = (y * (r * jax.nn.sigmoid(r))).astype(o_ref.dtype)


def _gla_scan(main, lr, wdec, bdec, seq, reverse, o_bwd=None, g_norm=None):
    t = main.shape[0]
    bsz = t // seq
    bt = min(512, seq)
    nb = seq // bt

    def row(b, n):
        return b * nb + ((nb - 1 - n) if reverse else n)

    in_specs = [
        pl.BlockSpec((bt, GLA_KDIM), lambda b, n: (row(b, n), 0)),
        pl.BlockSpec((bt, GLA_KDIM), lambda b, n: (row(b, n), 1)),
        pl.BlockSpec((bt, GLA_VDIM), lambda b, n: (row(b, n), 1)),
        pl.BlockSpec((bt, LANES), lambda b, n: (row(b, n), 0)),
        pl.BlockSpec((LANES, GLA_KDIM), lambda b, n: (0, 0)),
        pl.BlockSpec((1, GLA_KDIM), lambda b, n: (0, 0)),
    ]
    args = [main, main, main, lr, wdec, bdec]
    scratch = [pltpu.VMEM((GLA_HEADS, HEAD_K, HEAD_V), F32)]
    if reverse:
        out_dtype = F32
    else:
        in_specs += [
            pl.BlockSpec((bt, GLA_VDIM), lambda b, n: (row(b, n), 0)),
            pl.BlockSpec((bt, GLA_VDIM), lambda b, n: (row(b, n), 2)),
            pl.BlockSpec((1, HEAD_V), lambda b, n: (0, 0)),
        ]
        args += [o_bwd, main, g_norm.reshape(1, HEAD_V)]
        scratch.append(pltpu.VMEM((bt, GLA_VDIM), F32))
        out_dtype = BF16
    return pl.pallas_call(
        functools.partial(_gla_kernel, reverse=reverse, bt=bt),
        grid=(bsz, nb),
        in_specs=in_specs,
        out_specs=pl.BlockSpec((bt, GLA_VDIM), lambda b, n: (row(b, n), 0)),
        out_shape=jax.ShapeDtypeStruct((t, GLA_VDIM), out_dtype),
        scratch_shapes=scratch,
        compiler_params=_cparams(("parallel", "arbitrary")),
        name="gla_bwd" if reverse else "gla_fwd",
    )(*args)


def _pool_tile(up_ref, uc_ref, un_ref, w_ref, b_ref, s_ref, ext_ref, *, tm, tiles_per_seq, seq):
    it = pl.program_id(0) % tiles_per_seq
    prev = up_ref[...].astype(F32)
    nxt = un_ref[...].astype(F32)
    ext_ref[0:POOL_HALO, :] = jnp.where(it == 0, 0.0, prev)
    ext_ref[POOL_HALO:POOL_HALO + tm, :] = uc_ref[...].astype(F32)
    ext_ref[POOL_HALO + tm:, :] = jnp.where(it == tiles_per_seq - 1, 0.0, nxt)
    pos = it * tm + lax.broadcasted_iota(jnp.int32, (tm, POOL_GROUP_DIM), 0)
    outs = []
    for gi, w in enumerate(POOL_WINDOWS):
        cs = slice(gi * POOL_GROUP_DIM, (gi + 1) * POOL_GROUP_DIM)
        start = POOL_HALO - w // 2
        acc = ext_ref[start:start + tm, cs]
        for j in range(1, w):
            acc = acc + ext_ref[start + j:start + j + tm, cs]
        cnt = jnp.minimum(pos + w // 2, seq) - jnp.maximum(pos - w // 2, 0)
        p = acc / cnt.astype(F32) - ext_ref[POOL_HALO:POOL_HALO + tm, cs]
        y = _dot(p.astype(BF16), w_ref[0, gi])
        outs.append(((y + b_ref[0, :, cs]) * s_ref[0, :, cs]).astype(BF16))
    return jnp.concatenate(outs, axis=1)


def _merge_kernel(a_ref, up_ref, uc_ref, un_ref, pw_ref, pb_ref, ps_ref, ga_ref, gb_ref, x_ref,
                  wa_ref, wb_ref, wo_ref, g1_ref, fg_ref, sc_ref, sh_ref, wrh_ref, wrl_ref, br_ref,
                  xo_ref, h_ref, rt_ref, ext_ref, *, tm, tiles_per_seq, seq):
    b_in = _pool_tile(up_ref, uc_ref, un_ref, pw_ref, pb_ref, ps_ref, ext_ref,
                      tm=tm, tiles_per_seq=tiles_per_seq, seq=seq)
    ya = _dot(a_ref[...], wa_ref[0])
    yb = _dot(b_in, wb_ref[0])
    merged = jax.nn.sigmoid(ga_ref[...].astype(F32)) * ya + jax.nn.sigmoid(gb_ref[...].astype(F32)) * yb
    xn = x_ref[...] + g1_ref[0] * _dot(merged.astype(BF16), wo_ref[0])
    xo_ref[...] = xn
    ms = jnp.mean(xn * xn, axis=-1, keepdims=True)
    h = xn * lax.rsqrt(ms + EPS) * fg_ref[...] * (1.0 + sc_ref[0]) + sh_ref[0]
    h_ref[...] = h

    hh, hl = _split2(h)
    lg = _dot(hh, wrh_ref[0]) + (_dot(hh, wrl_ref[0]) + _dot(hl, wrh_ref[0])) + br_ref[0]
    lane = lax.broadcasted_iota(jnp.int32, lg.shape, 1).astype(F32)
    neg = -jnp.inf
    far = float(LANES)
    coarse = lane < N_GROUPS
    cmax = jnp.max(jnp.where(coarse, lg, neg), axis=-1, keepdims=True)
    g_star = jnp.min(jnp.where(coarse & (lg == cmax), lane, far), axis=-1, keepdims=True)
    p_g = 1.0 / jnp.sum(jnp.where(coarse, jnp.exp(lg - cmax), 0.0), axis=-1, keepdims=True)
    lo = N_GROUPS + EXPERTS_PER_GROUP * g_star
    fine = (lane >= lo) & (lane < lo + EXPERTS_PER_GROUP)
    v1 = jnp.max(jnp.where(fine, lg, neg), axis=-1, keepdims=True)
    i1 = jnp.min(jnp.where(fine & (lg == v1), lane, far), axis=-1, keepdims=True)
    fine2 = fine & (lane != i1)
    v2 = jnp.max(jnp.where(fine2, lg, neg), axis=-1, keepdims=True)
    i2 = jnp.min(jnp.where(fine2 & (lg == v2), lane, far), axis=-1, keepdims=True)
    e = jnp.exp(v2 - v1)
    w1 = p_g / (1.0 + e)
    w2 = p_g * e / (1.0 + e)
    rt = jnp.where(lane == 0, i1 - N_GROUPS,
                   jnp.where(lane == 1, i2 - N_GROUPS,
                             jnp.where(lane == 2, w1, jnp.where(lane == 3, w2, 0.0))))
    rt_ref[...] = rt


def _merge(a_in, main, x2, pool_w, pool_b, pool_s, wa, wb, wo, gate1, fg, scale2, shift2, wr_hi, wr_lo, br,
           layer, seq):
    t, d = x2.shape
    tm = min(256, seq)
    tiles_per_seq = seq // tm
    const = dict(pipeline_mode=pl.Buffered(1))
    bidx = lambda i: (i // tiles_per_seq, 0, 0)
    hb = tm // POOL_HALO
    n_hblocks = t // POOL_HALO
    u_col = 3
    n_win = len(POOL_WINDOWS)
    return pl.pallas_call(
        functools.partial(_merge_kernel, tm=tm, tiles_per_seq=tiles_per_seq, seq=seq),
        grid=(t // tm,),
        in_specs=[
            pl.BlockSpec((tm, GLA_VDIM), lambda i: (i, 0)),
            pl.BlockSpec((POOL_HALO, POOL_DIM), lambda i: (jnp.maximum(i * hb - 1, 0), u_col)),
            pl.BlockSpec((tm, POOL_DIM), lambda i: (i, u_col)),
            pl.BlockSpec((POOL_HALO, POOL_DIM), lambda i: (jnp.minimum((i + 1) * hb, n_hblocks - 1), u_col)),
            pl.BlockSpec((1, n_win, POOL_GROUP_DIM, POOL_GROUP_DIM), lambda i: (layer, 0, 0, 0)),
            pl.BlockSpec((1, 1, POOL_DIM), lambda i: (layer, 0, 0)),
            pl.BlockSpec((1, 1, POOL_DIM), lambda i: (layer, 0, 0)),
            pl.BlockSpec((tm, d), lambda i: (i, 2)),
            pl.BlockSpec((tm, d), lambda i: (i, 3)),
            pl.BlockSpec((tm, d), lambda i: (i, 0)),
            pl.BlockSpec((1, GLA_VDIM, d), lambda i: (layer, 0, 0), **const),
            pl.BlockSpec((1, POOL_DIM, d), lambda i: (layer, 0, 0), **const),
            pl.BlockSpec((1, d, d), lambda i: (layer, 0, 0), **const),
            pl.BlockSpec((1, 1, d), bidx),
            pl.BlockSpec((1, d), lambda i: (0, 0)),
            pl.BlockSpec((1, 1, d), bidx),
            pl.BlockSpec((1, 1, d), bidx),
            pl.BlockSpec((1, d, LANES), lambda i: (layer, 0, 0), **const),
            pl.BlockSpec((1, d, LANES), lambda i: (layer, 0, 0), **const),
            pl.BlockSpec((1, 1, LANES), lambda i: (layer, 0, 0)),
        ],
        out_specs=[
            pl.BlockSpec((tm, d), lambda i: (i, 0)),
            pl.BlockSpec((tm, d), lambda i: (i, 0)),
            pl.BlockSpec((tm, LANES), lambda i: (i, 0)),
        ],
        out_shape=[
            jax.ShapeDtypeStruct((t, d), F32),
            jax.ShapeDtypeStruct((t, d), F32),
            jax.ShapeDtypeStruct((t, LANES), F32),
        ],
        scratch_shapes=[pltpu.VMEM((tm + 2 * POOL_HALO, POOL_DIM), F32)],
        compiler_params=_cparams(("parallel",)),
        name="merge",
    )(a_in, main, main, main, pool_w, pool_b, pool_s, main, main, x2, wa, wb, wo, gate1, fg.reshape(1, d),
      scale2, shift2, wr_hi, wr_lo, br)


def _moe_kernel(be_ref, nv_ref, j0_ref, src_ref, dst_ref, h_hbm, w1_ref, w3_ref, w2_ref, y_hbm,
                xbuf, ybuf, w1b, w3b, w2b, wview, gsem, ssem, *, n_blocks):
    b = pl.program_id(0)
    slot = b % 2

    def gather_copy(sl, src_row, g, u):
        return pltpu.make_async_copy(h_hbm.at[pl.ds(src_row, 1)], xbuf.at[sl, g, pl.ds(u, 1)], gsem.at[sl])

    def scatter_copy(sl, g, u, dst_row):
        return pltpu.make_async_copy(ybuf.at[sl, g, pl.ds(u, 1)], y_hbm.at[pl.ds(dst_row, 1)], ssem.at[sl])

    def issue_gather_group(base, sl, g):
        for u in range(ISSUE_UNROLL):
            gather_copy(sl, src_ref[base + g * ISSUE_UNROLL + u], g, u).start()

    def issue_scatter(blk, sl):
        base = j0_ref[blk]
        n = nv_ref[blk]
        full = n // ISSUE_UNROLL

        def group(g, carry):
            for u in range(ISSUE_UNROLL):
                scatter_copy(sl, g, u, dst_ref[base + g * ISSUE_UNROLL + u]).start()
            return carry
        lax.fori_loop(0, full, group, 0)

        def tail(i, carry):
            scatter_copy(sl, full, i - full * ISSUE_UNROLL, dst_ref[base + i]).start()
            return carry
        lax.fori_loop(full * ISSUE_UNROLL, n, tail, 0)

    def wait_rows(sem, n):
        pltpu.make_async_copy(wview.at[pl.ds(0, n)], wview.at[pl.ds(0, n)], sem).wait()

    def wait_gather(sl):
        wait_rows(gsem.at[sl], MOE_BLOCK)

    def wait_scatter(blk, sl):
        wait_rows(ssem.at[sl], nv_ref[blk])

    n_groups = MOE_BLOCK // ISSUE_UNROLL
    used = nv_ref[b] > 0
    prev_used = nv_ref[jnp.maximum(b - 1, 0)] > 0

    @pl.when(b == 0)
    def _():
        def first(g, carry):
            issue_gather_group(j0_ref[0], 0, g)
            return carry
        lax.fori_loop(0, n_groups, first, 0)

    @pl.when((b >= 2) & (nv_ref[jnp.maximum(b - 2, 0)] > 0))
    def _():
        wait_scatter(b - 2, slot)

    @pl.when(used)
    def _():
        prev_e = be_ref[jnp.maximum(b - 1, 0)]

        @pl.when((b == 0) | (be_ref[b] != prev_e))
        def _():
            w1b[...] = w1_ref[0, 0].astype(BF16)
            w3b[...] = w3_ref[0, 0].astype(BF16)
            w2b[...] = w2_ref[0, 0].astype(BF16)

        wait_gather(slot)
        next_base = j0_ref[jnp.minimum(b + 1, n_blocks - 1)]
        d = xbuf.shape[-1]
        pieces = n_groups // 2
        kc = d // pieces
        xb = xbuf[slot].reshape(MOE_BLOCK, d).astype(BF16)
        a = g = None
        for k in range(pieces):
            ks = slice(k * kc, (k + 1) * kc)
            pa = _dot(xb[:, ks], w1b[ks, :])
            pg = _dot(xb[:, ks], w3b[ks, :])
            a = pa if a is None else a + pa
            g = pg if g is None else g + pg
            issue_gather_group(next_base, 1 - slot, k)
        mid = (a * jax.nn.sigmoid(a) * g).astype(BF16)
        for n in range(pieces):
            ns = slice(n * kc, (n + 1) * kc)
            ybuf[slot, :, :, ns] = _dot(mid, w2b[:, ns]).reshape(n_groups, ISSUE_UNROLL, kc)
            issue_gather_group(next_base, 1 - slot, pieces + n)
        issue_scatter(b, slot)

    @pl.when(jnp.logical_not(used) & (b > 0) & prev_used)
    def _():
        wait_gather(slot)

    @pl.when(b == n_blocks - 1)
    def _():
        @pl.when(used)
        def _():
            wait_gather(1 - slot)

        @pl.when((b >= 1) & prev_used)
        def _():
            wait_scatter(b - 1, 1 - slot)

        @pl.when(used)
        def _():
            wait_scatter(b, slot)


def _moe_experts(h2, block_expert, n_valid, j0, src_rows, dst_rows, w1, w3, w2, layer):
    t, d = h2.shape
    n_blocks = block_expert.shape[0]
    f = w1.shape[-1]
    grid_spec = pltpu.PrefetchScalarGridSpec(
        num_scalar_prefetch=5,
        grid=(n_blocks,),
        in_specs=[
            pl.BlockSpec(memory_space=pl.ANY),
            pl.BlockSpec((1, 1, d, f), lambda b, be, nv, j0, sr, ds: (layer, be[b], 0, 0)),
            pl.BlockSpec((1, 1, d, f), lambda b, be, nv, j0, sr, ds: (layer, be[b], 0, 0)),
            pl.BlockSpec((1, 1, f, d), lambda b, be, nv, j0, sr, ds: (layer, be[b], 0, 0)),
        ],
        out_specs=pl.BlockSpec(memory_space=pl.ANY),
        scratch_shapes=[
            pltpu.VMEM((2, MOE_BLOCK // ISSUE_UNROLL, ISSUE_UNROLL, d), F32),
            pltpu.VMEM((2, MOE_BLOCK // ISSUE_UNROLL, ISSUE_UNROLL, d), F32),
            pltpu.VMEM((d, f), BF16),
            pltpu.VMEM((d, f), BF16),
            pltpu.VMEM((f, d), BF16),
            pltpu.VMEM((MOE_BLOCK, d // LANES, LANES), F32),
            pltpu.SemaphoreType.DMA((2,)),
            pltpu.SemaphoreType.DMA((2,)),
        ],
    )
    return pl.pallas_call(
        functools.partial(_moe_kernel, n_blocks=n_blocks),
        grid_spec=grid_spec,
        out_shape=jax.ShapeDtypeStruct((TOP_K_FINE * t, d), F32),
        compiler_params=_cparams(("arbitrary",)),
        name="moe_experts",
    )(block_expert, n_valid, j0, src_rows, dst_rows, h2, w1, w3, w2)


def _route_plan(route, t):
    n_assign = t * TOP_K_FINE
    n_blocks = -(-n_assign // MOE_BLOCK) + N_EXPERTS
    e_flat = route[:, :TOP_K_FINE].astype(jnp.int32).reshape(-1)
    eids = jnp.arange(N_EXPERTS, dtype=jnp.int32)
    counts = jnp.sum((e_flat[:, None] == eids[None, :]).astype(jnp.int32), axis=0)
    start = jnp.cumsum(counts) - counts
    nblk = (counts + MOE_BLOCK - 1) // MOE_BLOCK
    bend = jnp.cumsum(nblk)
    bstart = bend - nblk
    order = jnp.argsort(e_flat, stable=True).astype(jnp.int32)
    blocks = jnp.arange(n_blocks, dtype=jnp.int32)[:, None]
    owner = ((blocks >= bstart[None, :]) & (blocks < bend[None, :])).astype(jnp.int32)
    block_expert = jnp.minimum(jnp.sum((blocks >= bend[None, :]).astype(jnp.int32), axis=1), N_EXPERTS - 1)
    in_expert = blocks[:, 0] - jnp.sum(owner * bstart[None, :], axis=1)
    n_valid = jnp.clip(jnp.sum(owner * counts[None, :], axis=1) - in_expert * MOE_BLOCK, 0, MOE_BLOCK)
    j0 = jnp.minimum(jnp.sum(owner * start[None, :], axis=1) + in_expert * MOE_BLOCK, n_assign)
    pad = jnp.zeros((MOE_BLOCK,), jnp.int32)
    src_rows = jnp.concatenate([order // TOP_K_FINE, pad])
    dst_rows = jnp.concatenate([(order % TOP_K_FINE) * t + order // TOP_K_FINE, pad])
    return block_expert, n_valid, j0, src_rows, dst_rows


def _final_kernel(x_ref, y0_ref, y1_ref, rt_ref, g2_ref, fg_ref, o_ref):
    xn = _moe_residual(x_ref, y0_ref, y1_ref, rt_ref, g2_ref)
    ms = jnp.mean(xn * xn, axis=-1, keepdims=True)
    o_ref[...] = xn * lax.rsqrt(ms + EPS) * fg_ref[...]


def _final_norm(x2, y, route, gate2, final_g, seq):
    t, d = x2.shape
    tm = min(256, seq)
    tiles_per_seq = seq // tm
    n_tiles = t // tm
    return pl.pallas_call(
        _final_kernel,
        grid=(n_tiles,),
        in_specs=[
            pl.BlockSpec((tm, d), lambda i: (i, 0)),
            pl.BlockSpec((tm, d), lambda i: (i, 0)),
            pl.BlockSpec((tm, d), lambda i: (i + n_tiles, 0)),
            pl.BlockSpec((tm, LANES), lambda i: (i, 0)),
            pl.BlockSpec((1, 1, d), lambda i: (i // tiles_per_seq, 0, 0)),
            pl.BlockSpec((1, d), lambda i: (0, 0)),
        ],
        out_specs=pl.BlockSpec((tm, d), lambda i: (i, 0)),
        out_shape=jax.ShapeDtypeStruct((t, d), F32),
        compiler_params=_cparams(("parallel",)),
        name="final_norm",
    )(x2, y, y, route, gate2, final_g.reshape(1, d))


def _prep_in_w(in_w):
    lr0 = 2 * GLA_KDIM + 2 * GLA_VDIM
    lr1 = lr0 + 2 * DECAY_RANK
    w_main = jnp.concatenate([in_w[:, :, :lr0], in_w[:, :, lr1:]], axis=2).astype(BF16)
    w_lr = jnp.pad(in_w[:, :, lr0:lr1].astype(BF16), ((0, 0), (0, 0), (0, LANES - 2 * DECAY_RANK)))
    return w_main, w_lr


def _prep_router(wc, bc, wf, bf):
    pad = LANES - N_GROUPS - N_EXPERTS
    wr = jnp.pad(jnp.concatenate([wc, wf], axis=2), ((0, 0), (0, 0), (0, pad)))
    wr_hi = wr.astype(BF16)
    wr_lo = (wr - wr_hi.astype(F32)).astype(BF16)
    br = jnp.pad(jnp.concatenate([bc, bf], axis=1), ((0, 0), (0, pad)))[:, None, :]
    return wr_hi, wr_lo, br


def kernel(x, c, ada_w, ada_b, mix_norm_g, in_w, decay_fw_w, decay_fw_b, decay_bw_w, decay_bw_b, gla_norm_g, pool_w, pool_b, pool_scale, branch_a_w, branch_b_w, out_w, ffn_norm_g, router_coarse_w, router_coarse_b, router_fine_w, router_fine_b, expert_w1, expert_w3, expert_w2, final_norm_g):
    bsz, seq, d = x.shape
    t = bsz * seq
    n_layers = ada_w.shape[0]
    mod = _ada_mod(c, ada_w, ada_b)
    w_main, w_lr = _prep_in_w(in_w)
    wr_hi, wr_lo, br = _prep_router(router_coarse_w, router_coarse_b, router_fine_w, router_fine_b)
    wa, wb, wo = branch_a_w.astype(BF16), branch_b_w.astype(BF16), out_w.astype(BF16)
    rank_pad = ((0, 0), (0, LANES - 2 * DECAY_RANK), (0, 0))
    wdec_f = jnp.pad(jnp.concatenate([decay_fw_w, jnp.zeros_like(decay_fw_w)], axis=1), rank_pad)
    wdec_b = jnp.pad(jnp.concatenate([jnp.zeros_like(decay_bw_w), decay_bw_w], axis=1), rank_pad)
    pool_wb = pool_w.astype(BF16)
    pool_b3, pool_s3 = pool_b[:, None, :], pool_scale[:, None, :]

    x2 = x.reshape(t, d)
    moe = None
    for l in range(n_layers):
        shift1, scale1, gate1, shift2, scale2, gate2 = [mod[l, :, i] for i in range(N_MOD)]
        if moe is None:
            main, lr = _in_proj(x2, mix_norm_g[l], scale1, shift1, w_main, w_lr, l, seq)
        else:
            main, lr, x2 = _in_proj(x2, mix_norm_g[l], scale1, shift1, w_main, w_lr, l, seq, moe=moe)
        o_bwd = _gla_scan(main, lr, wdec_b[l], decay_bw_b[l].reshape(1, GLA_KDIM), seq, reverse=True)
        a_in = _gla_scan(main, lr, wdec_f[l], decay_fw_b[l].reshape(1, GLA_KDIM), seq, reverse=False,
                         o_bwd=o_bwd, g_norm=gla_norm_g[l])
        x2, h2, route = _merge(a_in, main, x2, pool_wb, pool_b3, pool_s3, wa, wb, wo, gate1, ffn_norm_g[l],
                               scale2, shift2, wr_hi, wr_lo, br, l, seq)
        block_expert, n_valid, j0, src_rows, dst_rows = _route_plan(route, t)
        y = _moe_experts(h2, block_expert, n_valid, j0, src_rows, dst_rows, expert_w1, expert_w3, expert_w2, l)
        moe = (y, route, gate2)
    out = _final_norm(x2, *moe, final_norm_g, seq)
    return out.reshape(bsz, seq, d)
```

```python
import functools

import jax
import jax.numpy as jnp
from jax import lax
from jax.experimental import pallas as pl
from jax.experimental.pallas import tpu as pltpu

F32 = jnp.float32
BF16 = jnp.bfloat16

D_MODEL = 2048
GLA_HEADS = 4
GLA_VDIM = 1024
GLA_KDIM = 512
HEAD_K = 128
HEAD_V = 256
DECAY_RANK = 16
GATE_NORMALIZER = 16.0
LOG2_E = 1.4426950408889634
CHUNK = 128
CUMSUM_ROWS = 256
POOL_DIM = 1024
POOL_WINDOWS = (2, 4, 8, 16)
POOL_GROUP_DIM = 256
POOL_HALO = 16
N_GROUPS = 8
EXPERTS_PER_GROUP = 8
N_EXPERTS = 64
TOP_K_FINE = 2
D_FF_EXPERT = 256
MOE_BLOCK = 128
ISSUE_UNROLL = 8
N_MOD = 6
EPS = 1e-6
LANES = 128
MAIN_COLS = 8192
VMEM_LIMIT = 56 * 1024 * 1024


def _cparams(sem):
    return pltpu.CompilerParams(dimension_semantics=sem, vmem_limit_bytes=VMEM_LIMIT)


def _split2(a):
    hi = a.astype(BF16)
    lo = (a - hi.astype(F32)).astype(BF16)
    return hi, lo


def _split3(a):
    p1 = a.astype(BF16)
    r1 = a - p1.astype(F32)
    p2 = r1.astype(BF16)
    p3 = (r1 - p2.astype(F32)).astype(BF16)
    return p1, p2, p3


def _dot(a, b):
    return jnp.dot(a, b, preferred_element_type=F32)


HIGH_HALF = 0xFFFF0000


def _pack_bf16_pair(lo, hi):
    lo = lax.bitcast_convert_type(lo.astype(BF16).astype(F32), jnp.uint32)
    hi = lax.bitcast_convert_type(hi.astype(BF16).astype(F32), jnp.uint32)
    return (lo >> 16) | (hi & jnp.uint32(HIGH_HALF))


def _pack_bf16_halves(v):
    n = v.shape[1] // 2
    return _pack_bf16_pair(v[:, :n], v[:, n:])


def _unpack_bf16_halves(w):
    lo = lax.bitcast_convert_type(w << 16, F32)
    hi = lax.bitcast_convert_type(w & jnp.uint32(HIGH_HALF), F32)
    return jnp.concatenate([lo, hi], axis=1)


def _dot3(a, b):
    ah, al = _split2(a)
    bh, bl = _split2(b)
    return _dot(ah, bh) + (_dot(ah, bl) + _dot(al, bh))


def _ada_kernel(c_ref, w_ref, b_ref, o_ref):
    c = c_ref[...]
    s = (c * jax.nn.sigmoid(c)).astype(BF16)
    o_ref[0] = _dot(s, w_ref[0].astype(BF16)) + b_ref[0]


def _ada_mod(c, ada_w, ada_b):
    n_layers, d, n = ada_w.shape
    bsz = c.shape[0]
    rows = 8
    tn = 1024
    c_pad = jnp.zeros((rows, d), F32).at[:bsz].set(c)
    out = pl.pallas_call(
        _ada_kernel,
        grid=(n_layers, n // tn),
        in_specs=[
            pl.BlockSpec((rows, d), lambda l, j: (0, 0)),
            pl.BlockSpec((1, d, tn), lambda l, j: (l, 0, j)),
            pl.BlockSpec((1, 1, tn), lambda l, j: (l, 0, j)),
        ],
        out_specs=pl.BlockSpec((1, rows, tn), lambda l, j: (l, 0, j)),
        out_shape=jax.ShapeDtypeStruct((n_layers, rows, n), F32),
        compiler_params=_cparams(("parallel", "parallel")),
        name="ada_mod",
    )(c_pad, ada_w, ada_b.reshape(n_layers, 1, n))
    return out[:, :bsz].reshape(n_layers, bsz, N_MOD, 1, d)


def _inproj_kernel(x_ref, g_ref, sc_ref, sh_ref, w_ref, wlr_ref, o_ref, lr_ref, h_scr):
    @pl.when(pl.program_id(1) == 0)
    def _():
        x = x_ref[...]
        ms = jnp.mean(x * x, axis=-1, keepdims=True)
        y = x * lax.rsqrt(ms + EPS) * g_ref[...]
        hb = (y * (1.0 + sc_ref[0]) + sh_ref[0]).astype(BF16)
        h_scr[...] = hb
        lr_ref[...] = _dot(hb, wlr_ref[0])

    o_ref[...] = _dot(h_scr[...], w_ref[0]).astype(o_ref.dtype)


def _in_proj(x2, g, scale, shift, w_main, w_lr, layer, seq):
    t, d = x2.shape
    tm = min(1024, seq)
    tn = 1024
    tiles_per_seq = seq // tm
    bidx = lambda i, j: (i // tiles_per_seq, 0, 0)
    return pl.pallas_call(
        _inproj_kernel,
        grid=(t // tm, MAIN_COLS // tn),
        in_specs=[
            pl.BlockSpec((tm, d), lambda i, j: (i, 0)),
            pl.BlockSpec((1, d), lambda i, j: (0, 0)),
            pl.BlockSpec((1, 1, d), bidx),
            pl.BlockSpec((1, 1, d), bidx),
            pl.BlockSpec((1, d, tn), lambda i, j: (layer, 0, j)),
            pl.BlockSpec((1, d, LANES), lambda i, j: (layer, 0, 0)),
        ],
        out_specs=[
            pl.BlockSpec((tm, tn), lambda i, j: (i, j)),
            pl.BlockSpec((tm, LANES), lambda i, j: (i, 0)),
        ],
        out_shape=[
            jax.ShapeDtypeStruct((t, MAIN_COLS), BF16),
            jax.ShapeDtypeStruct((t, LANES), F32),
        ],
        scratch_shapes=[pltpu.VMEM((tm, d), BF16)],
        compiler_params=_cparams(("parallel", "arbitrary")),
        name="in_proj",
    )(x2, g.reshape(1, d), scale, shift, w_main, w_lr)


def _gla_kernel(*refs, reverse, bt):
    if reverse:
        q_ref, k_ref, v_ref, lr_ref, wd_ref, bd_ref, o_ref, st_ref = refs
    else:
        (q_ref, k_ref, v_ref, lr_ref, wd_ref, bd_ref, ob_ref, r_ref, gn_ref,
         o_ref, st_ref, o_scr) = refs

    @pl.when(pl.program_id(1) == 0)
    def _():
        st_ref[...] = jnp.zeros_like(st_ref)

    z = _dot3(lr_ref[...], wd_ref[...]) + bd_ref[...]
    la = (jnp.minimum(z, 0.0) - jnp.log(1.0 + jnp.exp(-jnp.abs(z)))) * (LOG2_E / GATE_NORMALIZER)

    tb = min(bt, CUMSUM_ROWS)
    rows = lax.broadcasted_iota(jnp.int32, (tb, tb), 0)
    cols = lax.broadcasted_iota(jnp.int32, (tb, tb), 1)
    same_chunk = (rows // CHUNK) == (cols // CHUNK)
    ordered = (cols >= rows) if reverse else (cols <= rows)
    tri = jnp.where(same_chunk & ordered, 1.0, 0.0).astype(BF16)
    l1, l2, l3 = _split3(la)
    cum = jnp.concatenate(
        [_dot(tri, l1[s:s + tb]) + (_dot(tri, l2[s:s + tb]) + _dot(tri, l3[s:s + tb])) for s in range(0, bt, tb)],
        axis=0)

    crow = lax.broadcasted_iota(jnp.int32, (CHUNK, CHUNK), 0)
    ccol = lax.broadcasted_iota(jnp.int32, (CHUNK, CHUNK), 1)
    att_mask = (ccol > crow) if reverse else (ccol <= crow)

    n_chunks = bt // CHUNK
    chunk_order = range(n_chunks - 1, -1, -1) if reverse else range(n_chunks)
    q_scale = HEAD_K ** -0.5
    for c in chunk_order:
        sl = slice(c * CHUNK, (c + 1) * CHUNK)
        for h in range(GLA_HEADS):
            hk = slice(h * HEAD_K, (h + 1) * HEAD_K)
            hv = slice(h * HEAD_V, (h + 1) * HEAD_V)
            state = st_ref[h]
            cm = cum[sl, hk]
            last = cm[0:1] if reverse else cm[CHUNK - 1:CHUNK]
            mid = cm[CHUNK // 2:CHUNK // 2 + 1]
            qf = q_ref[sl, hk].astype(F32) * q_scale
            kf = k_ref[sl, hk].astype(F32)
            vb = v_ref[sl, hv]
            q_in = (qf * jnp.exp2(cm)).astype(BF16)
            q_att = (qf * jnp.exp2(cm - mid)).astype(BF16)
            k_att = (kf * jnp.exp2(mid - cm)).astype(BF16)
            k_end = (kf * jnp.exp2(last - cm)).astype(BF16)
            att = lax.dot_general(q_att, k_att, (((1,), (1,)), ((), ())), preferred_element_type=F32)
            att = jnp.where(att_mask, att, 0.0).astype(BF16)
            o = _dot(att, vb) + _dot(q_in, state.astype(BF16))
            if reverse:
                o_ref[sl, hv] = o
            else:
                o_scr[sl, hv] = o
            dec_col = jnp.transpose(jnp.broadcast_to(jnp.exp2(last), (HEAD_K, HEAD_K)))
            dec = jnp.concatenate([dec_col, dec_col], axis=1)
            kv = lax.dot_general(k_end, vb, (((0,), (0,)), ((), ())), preferred_element_type=F32)
            st_ref[h] = dec * state + kv

    if not reverse:
        for h in range(GLA_HEADS):
            hv = slice(h * HEAD_V, (h + 1) * HEAD_V)
            o = o_scr[:, hv] + ob_ref[:, hv]
            ms = jnp.mean(o * o, axis=-1, keepdims=True)
            y = o * lax.rsqrt(ms + EPS) * gn_ref[...]
            r = r_ref[:, hv].astype(F32)
            o_ref[:, hv] = (y * (r * jax.nn.sigmoid(r))).astype(o_ref.dtype)


def _gla_scan(main, lr, wdec, bdec, seq, reverse, o_bwd=None, g_norm=None):
    t = main.shape[0]
    bsz = t // seq
    bt = min(512, seq)
    nb = seq // bt

    def row(b, n):
        return b * nb + ((nb - 1 - n) if reverse else n)

    in_specs = [
        pl.BlockSpec((bt, GLA_KDIM), lambda b, n: (row(b, n), 0)),
        pl.BlockSpec((bt, GLA_KDIM), lambda b, n: (row(b, n), 1)),
        pl.BlockSpec((bt, GLA_VDIM), lambda b, n: (row(b, n), 1)),
        pl.BlockSpec((bt, LANES), lambda b, n: (row(b, n), 0)),
        pl.BlockSpec((LANES, GLA_KDIM), lambda b, n: (0, 0)),
        pl.BlockSpec((1, GLA_KDIM), lambda b, n: (0, 0)),
    ]
    args = [main, main, main, lr, wdec, bdec]
    scratch = [pltpu.VMEM((GLA_HEADS, HEAD_K, HEAD_V), F32)]
    if reverse:
        out_dtype = F32
    else:
        in_specs += [
            pl.BlockSpec((bt, GLA_VDIM), lambda b, n: (row(b, n), 0)),
            pl.BlockSpec((bt, GLA_VDIM), lambda b, n: (row(b, n), 2)),
            pl.BlockSpec((1, HEAD_V), lambda b, n: (0, 0)),
        ]
        args += [o_bwd, main, g_norm.reshape(1, HEAD_V)]
        scratch.append(pltpu.VMEM((bt, GLA_VDIM), F32))
        out_dtype = BF16
    return pl.pallas_call(
        functools.partial(_gla_kernel, reverse=reverse, bt=bt),
        grid=(bsz, nb),
        in_specs=in_specs,
        out_specs=pl.BlockSpec((bt, GLA_VDIM), lambda b, n: (row(b, n), 0)),
        out_shape=jax.ShapeDtypeStruct((t, GLA_VDIM), out_dtype),
        scratch_shapes=scratch,
        compiler_params=_cparams(("parallel", "arbitrary")),
        name="gla_bwd" if reverse else "gla_fwd",
    )(*args)


def _pool_tile(up_ref, uc_ref, un_ref, w_ref, b_ref, s_ref, ext_ref, *, tm, tiles_per_seq, seq):
    it = pl.program_id(0) % tiles_per_seq
    prev = up_ref[...].astype(F32)
    nxt = un_ref[...].astype(F32)
    ext_ref[0:POOL_HALO, :] = jnp.where(it == 0, 0.0, prev)
    ext_ref[POOL_HALO:POOL_HALO + tm, :] = uc_ref[...].astype(F32)
    ext_ref[POOL_HALO + tm:, :] = jnp.where(it == tiles_per_seq - 1, 0.0, nxt)
    pos = it * tm + lax.broadcasted_iota(jnp.int32, (tm, POOL_GROUP_DIM), 0)
    outs = []
    for gi, w in enumerate(POOL_WINDOWS):
        cs = slice(gi * POOL_GROUP_DIM, (gi + 1) * POOL_GROUP_DIM)
        start = POOL_HALO - w // 2
        acc = ext_ref[start:start + tm, cs]
        for j in range(1, w):
            acc = acc + ext_ref[start + j:start + j + tm, cs]
        cnt = jnp.minimum(pos + w // 2, seq) - jnp.maximum(pos - w // 2, 0)
        p = acc / cnt.astype(F32) - ext_ref[POOL_HALO:POOL_HALO + tm, cs]
        y = _dot(p.astype(BF16), w_ref[0, gi])
        outs.append(((y + b_ref[0, :, cs]) * s_ref[0, :, cs]).astype(BF16))
    return jnp.concatenate(outs, axis=1)


def _merge_kernel(a_ref, up_ref, uc_ref, un_ref, pw_ref, pb_ref, ps_ref, ga_ref, gb_ref, x_ref,
                  wa_ref, wb_ref, wo_ref, g1_ref, fg_ref, sc_ref, sh_ref, wrh_ref, wrl_ref, br_ref,
                  xo_ref, h_ref, rt_ref, ext_ref, *, tm, tiles_per_seq, seq):
    b_in = _pool_tile(up_ref, uc_ref, un_ref, pw_ref, pb_ref, ps_ref, ext_ref,
                      tm=tm, tiles_per_seq=tiles_per_seq, seq=seq)
    ya = _dot(a_ref[...], wa_ref[0])
    yb = _dot(b_in, wb_ref[0])
    merged = jax.nn.sigmoid(ga_ref[...].astype(F32)) * ya + jax.nn.sigmoid(gb_ref[...].astype(F32)) * yb
    xn = x_ref[...] + g1_ref[0] * _dot(merged.astype(BF16), wo_ref[0])
    xo_ref[...] = xn
    ms = jnp.mean(xn * xn, axis=-1, keepdims=True)
    h = xn * lax.rsqrt(ms + EPS) * fg_ref[...] * (1.0 + sc_ref[0]) + sh_ref[0]
    h_ref[...] = _pack_bf16_halves(h)

    hh, hl = _split2(h)
    lg = _dot(hh, wrh_ref[0]) + (_dot(hh, wrl_ref[0]) + _dot(hl, wrh_ref[0])) + br_ref[0]
    lane = lax.broadcasted_iota(jnp.int32, lg.shape, 1).astype(F32)
    neg = -jnp.inf
    far = float(LANES)
    coarse = lane < N_GROUPS
    cmax = jnp.max(jnp.where(coarse, lg, neg), axis=-1, keepdims=True)
    g_star = jnp.min(jnp.where(coarse & (lg == cmax), lane, far), axis=-1, keepdims=True)
    p_g = 1.0 / jnp.sum(jnp.where(coarse, jnp.exp(lg - cmax), 0.0), axis=-1, keepdims=True)
    lo = N_GROUPS + EXPERTS_PER_GROUP * g_star
    fine = (lane >= lo) & (lane < lo + EXPERTS_PER_GROUP)
    v1 = jnp.max(jnp.where(fine, lg, neg), axis=-1, keepdims=True)
    i1 = jnp.min(jnp.where(fine & (lg == v1), lane, far), axis=-1, keepdims=True)
    fine2 = fine & (lane != i1)
    v2 = jnp.max(jnp.where(fine2, lg, neg), axis=-1, keepdims=True)
    i2 = jnp.min(jnp.where(fine2 & (lg == v2), lane, far), axis=-1, keepdims=True)
    e = jnp.exp(v2 - v1)
    w1 = p_g / (1.0 + e)
    w2 = p_g * e / (1.0 + e)
    rt = jnp.where(lane == 0, i1 - N_GROUPS,
                   jnp.where(lane == 1, i2 - N_GROUPS,
                             jnp.where(lane == 2, w1, jnp.where(lane == 3, w2, 0.0))))
    rt_ref[...] = rt


def _merge(a_in, main, x2, pool_w, pool_b, pool_s, wa, wb, wo, gate1, fg, scale2, shift2, wr_hi, wr_lo, br,
           layer, seq):
    t, d = x2.shape
    tm = min(256, seq)
    tiles_per_seq = seq // tm
    const = dict(pipeline_mode=pl.Buffered(1))
    bidx = lambda i: (i // tiles_per_seq, 0, 0)
    hb = tm // POOL_HALO
    n_hblocks = t // POOL_HALO
    u_col = 3
    n_win = len(POOL_WINDOWS)
    return pl.pallas_call(
        functools.partial(_merge_kernel, tm=tm, tiles_per_seq=tiles_per_seq, seq=seq),
        grid=(t // tm,),
        in_specs=[
            pl.BlockSpec((tm, GLA_VDIM), lambda i: (i, 0)),
            pl.BlockSpec((POOL_HALO, POOL_DIM), lambda i: (jnp.maximum(i * hb - 1, 0), u_col)),
            pl.BlockSpec((tm, POOL_DIM), lambda i: (i, u_col)),
            pl.BlockSpec((POOL_HALO, POOL_DIM), lambda i: (jnp.minimum((i + 1) * hb, n_hblocks - 1), u_col)),
            pl.BlockSpec((1, n_win, POOL_GROUP_DIM, POOL_GROUP_DIM), lambda i: (layer, 0, 0, 0)),
            pl.BlockSpec((1, 1, POOL_DIM), lambda i: (layer, 0, 0)),
            pl.BlockSpec((1, 1, POOL_DIM), lambda i: (layer, 0, 0)),
            pl.BlockSpec((tm, d), lambda i: (i, 2)),
            pl.BlockSpec((tm, d), lambda i: (i, 3)),
            pl.BlockSpec((tm, d), lambda i: (i, 0)),
            pl.BlockSpec((1, GLA_VDIM, d), lambda i: (layer, 0, 0), **const),
            pl.BlockSpec((1, POOL_DIM, d), lambda i: (layer, 0, 0), **const),
            pl.BlockSpec((1, d, d), lambda i: (layer, 0, 0), **const),
            pl.BlockSpec((1, 1, d), bidx),
            pl.BlockSpec((1, d), lambda i: (0, 0)),
            pl.BlockSpec((1, 1, d), bidx),
            pl.BlockSpec((1, 1, d), bidx),
            pl.BlockSpec((1, d, LANES), lambda i: (layer, 0, 0), **const),
            pl.BlockSpec((1, d, LANES), lambda i: (layer, 0, 0), **const),
            pl.BlockSpec((1, 1, LANES), lambda i: (layer, 0, 0)),
        ],
        out_specs=[
            pl.BlockSpec((tm, d), lambda i: (i, 0)),
            pl.BlockSpec((tm, d // 2), lambda i: (i, 0)),
            pl.BlockSpec((tm, LANES), lambda i: (i, 0)),
        ],
        out_shape=[
            jax.ShapeDtypeStruct((t, d), F32),
            jax.ShapeDtypeStruct((t, d // 2), jnp.uint32),
            jax.ShapeDtypeStruct((t, LANES), F32),
        ],
        scratch_shapes=[pltpu.VMEM((tm + 2 * POOL_HALO, POOL_DIM), F32)],
        compiler_params=_cparams(("parallel",)),
        name="merge",
    )(a_in, main, main, main, pool_w, pool_b, pool_s, main, main, x2, wa, wb, wo, gate1, fg.reshape(1, d),
      scale2, shift2, wr_hi, wr_lo, br)


def _moe_kernel(be_ref, nv_ref, j0_ref, src_ref, dst_ref, h_hbm, w1_ref, w3_ref, w2_ref, y_hbm,
                xbuf, ybuf, w1b, w3b, w2b, wview, gsem, ssem, *, n_blocks):
    b = pl.program_id(0)
    slot = b % 2

    def gather_copy(sl, src_row, g, u):
        return pltpu.make_async_copy(h_hbm.at[pl.ds(src_row, 1)], xbuf.at[sl, g, pl.ds(u, 1)], gsem.at[sl])

    def scatter_copy(sl, g, u, dst_row):
        return pltpu.make_async_copy(ybuf.at[sl, g, pl.ds(u, 1)], y_hbm.at[pl.ds(dst_row, 1)], ssem.at[sl])

    def issue_gather_group(base, sl, g):
        for u in range(ISSUE_UNROLL):
            gather_copy(sl, src_ref[base + g * ISSUE_UNROLL + u], g, u).start()

    def issue_scatter(blk, sl):
        base = j0_ref[blk]
        n = nv_ref[blk]
        full = n // ISSUE_UNROLL

        def group(g, carry):
            for u in range(ISSUE_UNROLL):
                scatter_copy(sl, g, u, dst_ref[base + g * ISSUE_UNROLL + u]).start()
            return carry
        lax.fori_loop(0, full, group, 0)

        def tail(i, carry):
            scatter_copy(sl, full, i - full * ISSUE_UNROLL, dst_ref[base + i]).start()
            return carry
        lax.fori_loop(full * ISSUE_UNROLL, n, tail, 0)

    def wait_rows(sem, n):
        pltpu.make_async_copy(wview.at[pl.ds(0, n)], wview.at[pl.ds(0, n)], sem).wait()

    def wait_gather(sl):
        wait_rows(gsem.at[sl], MOE_BLOCK)

    def wait_scatter(blk, sl):
        wait_rows(ssem.at[sl], nv_ref[blk])

    n_groups = MOE_BLOCK // ISSUE_UNROLL
    used = nv_ref[b] > 0
    prev_used = nv_ref[jnp.maximum(b - 1, 0)] > 0

    @pl.when(b == 0)
    def _():
        def first(g, carry):
            issue_gather_group(j0_ref[0], 0, g)
            return carry
        lax.fori_loop(0, n_groups, first, 0)

    @pl.when((b >= 2) & (nv_ref[jnp.maximum(b - 2, 0)] > 0))
    def _():
        wait_scatter(b - 2, slot)

    @pl.when(used)
    def _():
        prev_e = be_ref[jnp.maximum(b - 1, 0)]

        @pl.when((b == 0) | (be_ref[b] != prev_e))
        def _():
            w1b[...] = w1_ref[0, 0].astype(BF16)
            w3b[...] = w3_ref[0, 0].astype(BF16)
            w2b[...] = w2_ref[0, 0].astype(BF16)

        wait_gather(slot)
        next_base = j0_ref[jnp.minimum(b + 1, n_blocks - 1)]
        dh = xbuf.shape[-1]
        pieces = n_groups // 2
        kc = 2 * dh // pieces
        xb = _unpack_bf16_halves(xbuf[slot].reshape(MOE_BLOCK, dh)).astype(BF16)
        a = g = None
        for k in range(pieces):
            ks = slice(k * kc, (k + 1) * kc)
            pa = _dot(xb[:, ks], w1b[ks, :])
            pg = _dot(xb[:, ks], w3b[ks, :])
            a = pa if a is None else a + pa
            g = pg if g is None else g + pg
            issue_gather_group(next_base, 1 - slot, k)
        mid = (a * jax.nn.sigmoid(a) * g).astype(BF16)
        for n in range(pieces // 2):
            ns = slice(n * kc, (n + 1) * kc)
            ns_hi = slice(dh + n * kc, dh + (n + 1) * kc)
            packed = _pack_bf16_pair(_dot(mid, w2b[:, ns]), _dot(mid, w2b[:, ns_hi]))
            ybuf[slot, :, :, ns] = packed.reshape(n_groups, ISSUE_UNROLL, kc)
            issue_gather_group(next_base, 1 - slot, pieces + 2 * n)
            issue_gather_group(next_base, 1 - slot, pieces + 2 * n + 1)
        issue_scatter(b, slot)

    @pl.when(jnp.logical_not(used) & (b > 0) & prev_used)
    def _():
        wait_gather(slot)

    @pl.when(b == n_blocks - 1)
    def _():
        @pl.when(used)
        def _():
            wait_gather(1 - slot)

        @pl.when((b >= 1) & prev_used)
        def _():
            wait_scatter(b - 1, 1 - slot)

        @pl.when(used)
        def _():
            wait_scatter(b, slot)


def _moe_experts(h2, block_expert, n_valid, j0, src_rows, dst_rows, w1, w3, w2, layer):
    t, dh = h2.shape
    d = 2 * dh
    n_blocks = block_expert.shape[0]
    f = w1.shape[-1]
    row_buf = pltpu.VMEM((2, MOE_BLOCK // ISSUE_UNROLL, ISSUE_UNROLL, dh), jnp.uint32)
    grid_spec = pltpu.PrefetchScalarGridSpec(
        num_scalar_prefetch=5,
        grid=(n_blocks,),
        in_specs=[
            pl.BlockSpec(memory_space=pl.ANY),
            pl.BlockSpec((1, 1, d, f), lambda b, be, nv, j0, sr, ds: (layer, be[b], 0, 0)),
            pl.BlockSpec((1, 1, d, f), lambda b, be, nv, j0, sr, ds: (layer, be[b], 0, 0)),
            pl.BlockSpec((1, 1, f, d), lambda b, be, nv, j0, sr, ds: (layer, be[b], 0, 0)),
        ],
        out_specs=pl.BlockSpec(memory_space=pl.ANY),
        scratch_shapes=[
            row_buf,
            row_buf,
            pltpu.VMEM((d, f), BF16),
            pltpu.VMEM((d, f), BF16),
            pltpu.VMEM((f, d), BF16),
            pltpu.VMEM((MOE_BLOCK, dh // LANES, LANES), jnp.uint32),
            pltpu.SemaphoreType.DMA((2,)),
            pltpu.SemaphoreType.DMA((2,)),
        ],
    )
    return pl.pallas_call(
        functools.partial(_moe_kernel, n_blocks=n_blocks),
        grid_spec=grid_spec,
        out_shape=jax.ShapeDtypeStruct((TOP_K_FINE * t, dh), jnp.uint32),
        compiler_params=_cparams(("arbitrary",)),
        name="moe_experts",
    )(block_expert, n_valid, j0, src_rows, dst_rows, h2, w1, w3, w2)


def _route_plan(route, t):
    n_assign = t * TOP_K_FINE
    n_blocks = -(-n_assign // MOE_BLOCK) + N_EXPERTS
    e_flat = route[:, :TOP_K_FINE].astype(jnp.int32).reshape(-1)
    eids = jnp.arange(N_EXPERTS, dtype=jnp.int32)
    counts = jnp.sum((e_flat[:, None] == eids[None, :]).astype(jnp.int32), axis=0)
    start = jnp.cumsum(counts) - counts
    nblk = (counts + MOE_BLOCK - 1) // MOE_BLOCK
    bend = jnp.cumsum(nblk)
    bstart = bend - nblk
    order = jnp.argsort(e_flat, stable=True).astype(jnp.int32)
    blocks = jnp.arange(n_blocks, dtype=jnp.int32)[:, None]
    owner = ((blocks >= bstart[None, :]) & (blocks < bend[None, :])).astype(jnp.int32)
    block_expert = jnp.minimum(jnp.sum((blocks >= bend[None, :]).astype(jnp.int32), axis=1), N_EXPERTS - 1)
    in_expert = blocks[:, 0] - jnp.sum(owner * bstart[None, :], axis=1)
    n_valid = jnp.clip(jnp.sum(owner * counts[None, :], axis=1) - in_expert * MOE_BLOCK, 0, MOE_BLOCK)
    j0 = jnp.minimum(jnp.sum(owner * start[None, :], axis=1) + in_expert * MOE_BLOCK, n_assign)
    pad = jnp.zeros((MOE_BLOCK,), jnp.int32)
    src_rows = jnp.concatenate([order // TOP_K_FINE, pad])
    dst_rows = jnp.concatenate([(order % TOP_K_FINE) * t + order // TOP_K_FINE, pad])
    return block_expert, n_valid, j0, src_rows, dst_rows


def _combine_kernel(x_ref, y0_ref, y1_ref, rt_ref, g2_ref, fg_ref, o_ref, *, final):
    rt = rt_ref[...]
    y0 = _unpack_bf16_halves(y0_ref[...])
    y1 = _unpack_bf16_halves(y1_ref[...])
    xn = x_ref[...] + g2_ref[0] * (rt[:, 2:3] * y0 + rt[:, 3:4] * y1)
    if final:
        ms = jnp.mean(xn * xn, axis=-1, keepdims=True)
        xn = xn * lax.rsqrt(ms + EPS) * fg_ref[...]
    o_ref[...] = xn


def _combine(x2, y, route, gate2, final_g, seq, final):
    t, d = x2.shape
    tm = min(512, seq)
    tiles_per_seq = seq // tm
    n_tiles = t // tm
    return pl.pallas_call(
        functools.partial(_combine_kernel, final=final),
        grid=(n_tiles,),
        in_specs=[
            pl.BlockSpec((tm, d), lambda i: (i, 0)),
            pl.BlockSpec((tm, d // 2), lambda i: (i, 0)),
            pl.BlockSpec((tm, d // 2), lambda i: (i + n_tiles, 0)),
            pl.BlockSpec((tm, LANES), lambda i: (i, 0)),
            pl.BlockSpec((1, 1, d), lambda i: (i // tiles_per_seq, 0, 0)),
            pl.BlockSpec((1, d), lambda i: (0, 0)),
        ],
        out_specs=pl.BlockSpec((tm, d), lambda i: (i, 0)),
        out_shape=jax.ShapeDtypeStruct((t, d), F32),
        compiler_params=_cparams(("parallel",)),
        name="combine_final" if final else "combine",
    )(x2, y, y, route, gate2, final_g.reshape(1, d))


def _prep_in_w(in_w):
    lr0 = 2 * GLA_KDIM + 2 * GLA_VDIM
    lr1 = lr0 + 2 * DECAY_RANK
    w_main = jnp.concatenate([in_w[:, :, :lr0].astype(BF16), in_w[:, :, lr1:].astype(BF16)], axis=2)
    w_lr = jnp.pad(in_w[:, :, lr0:lr1].astype(BF16), ((0, 0), (0, 0), (0, LANES - 2 * DECAY_RANK)))
    return w_main, w_lr


def _prep_router(wc, bc, wf, bf):
    pad = LANES - N_GROUPS - N_EXPERTS
    wr = jnp.pad(jnp.concatenate([wc, wf], axis=2), ((0, 0), (0, 0), (0, pad)))
    wr_hi = wr.astype(BF16)
    wr_lo = (wr - wr_hi.astype(F32)).astype(BF16)
    br = jnp.pad(jnp.concatenate([bc, bf], axis=1), ((0, 0), (0, pad)))[:, None, :]
    return wr_hi, wr_lo, br


def kernel(x, c, ada_w, ada_b, mix_norm_g, in_w, decay_fw_w, decay_fw_b, decay_bw_w, decay_bw_b, gla_norm_g, pool_w, pool_b, pool_scale, branch_a_w, branch_b_w, out_w, ffn_norm_g, router_coarse_w, router_coarse_b, router_fine_w, router_fine_b, expert_w1, expert_w3, expert_w2, final_norm_g):
    bsz, seq, d = x.shape
    t = bsz * seq
    n_layers = ada_w.shape[0]
    mod = _ada_mod(c, ada_w, ada_b)
    w_main, w_lr = _prep_in_w(in_w)
    wr_hi, wr_lo, br = _prep_router(router_coarse_w, router_coarse_b, router_fine_w, router_fine_b)
    wa, wb, wo = branch_a_w.astype(BF16), branch_b_w.astype(BF16), out_w.astype(BF16)
    rank_pad = ((0, 0), (0, LANES - 2 * DECAY_RANK), (0, 0))
    wdec_f = jnp.pad(jnp.concatenate([decay_fw_w, jnp.zeros_like(decay_fw_w)], axis=1), rank_pad)
    wdec_b = jnp.pad(jnp.concatenate([jnp.zeros_like(decay_bw_w), decay_bw_w], axis=1), rank_pad)
    pool_wb = pool_w.astype(BF16)
    pool_b3, pool_s3 = pool_b[:, None, :], pool_scale[:, None, :]

    x2 = x.reshape(t, d)
    for l in range(n_layers):
        shift1, scale1, gate1, shift2, scale2, gate2 = [mod[l, :, i] for i in range(N_MOD)]
        main, lr = _in_proj(x2, mix_norm_g[l], scale1, shift1, w_main, w_lr, l, seq)
        o_bwd = _gla_scan(main, lr, wdec_b[l], decay_bw_b[l].reshape(1, GLA_KDIM), seq, reverse=True)
        a_in = _gla_scan(main, lr, wdec_f[l], decay_fw_b[l].reshape(1, GLA_KDIM), seq, reverse=False,
                         o_bwd=o_bwd, g_norm=gla_norm_g[l])
        x2, h2, route = _merge(a_in, main, x2, pool_wb, pool_b3, pool_s3, wa, wb, wo, gate1, ffn_norm_g[l],
                               scale2, shift2, wr_hi, wr_lo, br, l, seq)
        block_expert, n_valid, j0, src_rows, dst_rows = _route_plan(route, t)
        y = _moe_experts(h2, block_expert, n_valid, j0, src_rows, dst_rows, expert_w1, expert_w3, expert_w2, l)
        x2 = _combine(x2, y, route, gate2, final_norm_g, seq, final=(l == n_layers - 1))
    return x2.reshape(bsz, seq, d)
```

```python
import functools

import jax
import jax.numpy as jnp
from jax import lax
from jax.experimental import pallas as pl
from jax.experimental.pallas import tpu as pltpu

F32 = jnp.float32
BF16 = jnp.bfloat16

D_MODEL = 2048
GLA_HEADS = 4
GLA_VDIM = 1024
GLA_KDIM = 512
HEAD_K = 128
HEAD_V = 256
DECAY_RANK = 16
GATE_NORMALIZER = 16.0
LOG2_E = 1.4426950408889634
CHUNK = 128
CUMSUM_ROWS = 256
POOL_DIM = 1024
POOL_WINDOWS = (2, 4, 8, 16)
POOL_GROUP_DIM = 256
POOL_HALO = 16
N_GROUPS = 8
EXPERTS_PER_GROUP = 8
N_EXPERTS = 64
TOP_K_FINE = 2
D_FF_EXPERT = 256
MOE_BLOCK = 128
ISSUE_UNROLL = 8
DMA_PRIORITIES = 2
N_MOD = 6
EPS = 1e-6
LANES = 128
MAIN_COLS = 8192
VMEM_LIMIT = 56 * 1024 * 1024


def _cparams(sem):
    return pltpu.CompilerParams(dimension_semantics=sem, vmem_limit_bytes=VMEM_LIMIT)


def _split2(a):
    hi = a.astype(BF16)
    lo = (a - hi.astype(F32)).astype(BF16)
    return hi, lo


def _split3(a):
    p1 = a.astype(BF16)
    r1 = a - p1.astype(F32)
    p2 = r1.astype(BF16)
    p3 = (r1 - p2.astype(F32)).astype(BF16)
    return p1, p2, p3


def _dot(a, b):
    return jnp.dot(a, b, preferred_element_type=F32)


HIGH_HALF = 0xFFFF0000


def _pack_bf16_pair(lo, hi):
    lo = lax.bitcast_convert_type(lo.astype(BF16).astype(F32), jnp.uint32)
    hi = lax.bitcast_convert_type(hi.astype(BF16).astype(F32), jnp.uint32)
    return (lo >> 16) | (hi & jnp.uint32(HIGH_HALF))


def _pack_bf16_halves(v):
    n = v.shape[1] // 2
    return _pack_bf16_pair(v[:, :n], v[:, n:])


def _unpack_bf16_halves(w):
    lo = lax.bitcast_convert_type(w << 16, F32)
    hi = lax.bitcast_convert_type(w & jnp.uint32(HIGH_HALF), F32)
    return jnp.concatenate([lo, hi], axis=1)


def _dot3(a, b):
    ah, al = _split2(a)
    bh, bl = _split2(b)
    return _dot(ah, bh) + (_dot(ah, bl) + _dot(al, bh))


def _ada_kernel(c_ref, w_ref, b_ref, o_ref):
    c = c_ref[...]
    s = (c * jax.nn.sigmoid(c)).astype(BF16)
    o_ref[0] = _dot(s, w_ref[0].astype(BF16)) + b_ref[0]


def _ada_mod(c, ada_w, ada_b):
    n_layers, d, n = ada_w.shape
    bsz = c.shape[0]
    rows = 8
    tn = 2048
    c_pad = jnp.zeros((rows, d), F32).at[:bsz].set(c)
    out = pl.pallas_call(
        _ada_kernel,
        grid=(n_layers, n // tn),
        in_specs=[
            pl.BlockSpec((rows, d), lambda l, j: (0, 0)),
            pl.BlockSpec((1, d, tn), lambda l, j: (l, 0, j)),
            pl.BlockSpec((1, 1, tn), lambda l, j: (l, 0, j)),
        ],
        out_specs=pl.BlockSpec((1, rows, tn), lambda l, j: (l, 0, j)),
        out_shape=jax.ShapeDtypeStruct((n_layers, rows, n), F32),
        compiler_params=_cparams(("parallel", "parallel")),
        name="ada_mod",
    )(c_pad, ada_w, ada_b.reshape(n_layers, 1, n))
    return out[:, :bsz].reshape(n_layers, bsz, N_MOD, 1, d)


def _inproj_kernel(x_ref, g_ref, sc_ref, sh_ref, w_ref, wlr_ref, o_ref, lr_ref, h_scr):
    @pl.when(pl.program_id(1) == 0)
    def _():
        x = x_ref[...]
        ms = jnp.mean(x * x, axis=-1, keepdims=True)
        y = x * lax.rsqrt(ms + EPS) * g_ref[...]
        hb = (y * (1.0 + sc_ref[0]) + sh_ref[0]).astype(BF16)
        h_scr[...] = hb
        lr_ref[...] = _dot(hb, wlr_ref[0])

    o_ref[...] = _dot(h_scr[...], w_ref[0]).astype(o_ref.dtype)


def _in_proj(x2, g, scale, shift, w_main, w_lr, layer, seq):
    t, d = x2.shape
    tm = min(1024, seq)
    tn = 1024
    tiles_per_seq = seq // tm
    bidx = lambda i, j: (i // tiles_per_seq, 0, 0)
    return pl.pallas_call(
        _inproj_kernel,
        grid=(t // tm, MAIN_COLS // tn),
        in_specs=[
            pl.BlockSpec((tm, d), lambda i, j: (i, 0)),
            pl.BlockSpec((1, d), lambda i, j: (0, 0)),
            pl.BlockSpec((1, 1, d), bidx),
            pl.BlockSpec((1, 1, d), bidx),
            pl.BlockSpec((1, d, tn), lambda i, j: (layer, 0, j)),
            pl.BlockSpec((1, d, LANES), lambda i, j: (layer, 0, 0)),
        ],
        out_specs=[
            pl.BlockSpec((tm, tn), lambda i, j: (i, j)),
            pl.BlockSpec((tm, LANES), lambda i, j: (i, 0)),
        ],
        out_shape=[
            jax.ShapeDtypeStruct((t, MAIN_COLS), BF16),
            jax.ShapeDtypeStruct((t, LANES), F32),
        ],
        scratch_shapes=[pltpu.VMEM((tm, d), BF16)],
        compiler_params=_cparams(("parallel", "arbitrary")),
        name="in_proj",
    )(x2, g.reshape(1, d), scale, shift, w_main, w_lr)


def _gla_kernel(*refs, reverse, bt):
    if reverse:
        q_ref, k_ref, v_ref, lr_ref, wd_ref, bd_ref, o_ref, st_ref = refs
    else:
        (q_ref, k_ref, v_ref, lr_ref, wd_ref, bd_ref, ob_ref, r_ref, gn_ref,
         o_ref, st_ref, o_scr) = refs

    @pl.when(pl.program_id(1) == 0)
    def _():
        st_ref[...] = jnp.zeros_like(st_ref)

    z = _dot3(lr_ref[...], wd_ref[...]) + bd_ref[...]
    la = (jnp.minimum(z, 0.0) - jnp.log(1.0 + jnp.exp(-jnp.abs(z)))) * (LOG2_E / GATE_NORMALIZER)

    tb = min(bt, CUMSUM_ROWS)
    rows = lax.broadcasted_iota(jnp.int32, (tb, tb), 0)
    cols = lax.broadcasted_iota(jnp.int32, (tb, tb), 1)
    same_chunk = (rows // CHUNK) == (cols // CHUNK)
    ordered = (cols >= rows) if reverse else (cols <= rows)
    tri = jnp.where(same_chunk & ordered, 1.0, 0.0).astype(BF16)
    l1, l2, l3 = _split3(la)
    cum = jnp.concatenate(
        [_dot(tri, l1[s:s + tb]) + (_dot(tri, l2[s:s + tb]) + _dot(tri, l3[s:s + tb])) for s in range(0, bt, tb)],
        axis=0)

    crow = lax.broadcasted_iota(jnp.int32, (CHUNK, CHUNK), 0)
    ccol = lax.broadcasted_iota(jnp.int32, (CHUNK, CHUNK), 1)
    att_mask = (ccol > crow) if reverse else (ccol <= crow)

    n_chunks = bt // CHUNK
    chunk_order = range(n_chunks - 1, -1, -1) if reverse else range(n_chunks)
    q_scale = HEAD_K ** -0.5
    for c in chunk_order:
        sl = slice(c * CHUNK, (c + 1) * CHUNK)
        for h in range(GLA_HEADS):
            hk = slice(h * HEAD_K, (h + 1) * HEAD_K)
            hv = slice(h * HEAD_V, (h + 1) * HEAD_V)
            state = st_ref[h]
            cm = cum[sl, hk]
            last = cm[0:1] if reverse else cm[CHUNK - 1:CHUNK]
            mid = cm[CHUNK // 2:CHUNK // 2 + 1]
            qf = q_ref[sl, hk].astype(F32) * q_scale
            kf = k_ref[sl, hk].astype(F32)
            vb = v_ref[sl, hv]
            q_in = (qf * jnp.exp2(cm)).astype(BF16)
            q_att = (qf * jnp.exp2(cm - mid)).astype(BF16)
            k_att = (kf * jnp.exp2(mid - cm)).astype(BF16)
            k_end = (kf * jnp.exp2(last - cm)).astype(BF16)
            att = lax.dot_general(q_att, k_att, (((1,), (1,)), ((), ())), preferred_element_type=F32)
            att = jnp.where(att_mask, att, 0.0).astype(BF16)
            o = _dot(att, vb) + _dot(q_in, state.astype(BF16))
            if reverse:
                o_ref[sl, hv] = o
            else:
                o_scr[sl, hv] = o
            dec_col = jnp.transpose(jnp.broadcast_to(jnp.exp2(last), (HEAD_K, HEAD_K)))
            dec = jnp.concatenate([dec_col, dec_col], axis=1)
            kv = lax.dot_general(k_end, vb, (((0,), (0,)), ((), ())), preferred_element_type=F32)
            st_ref[h] = dec * state + kv

    if not reverse:
        for h in range(GLA_HEADS):
            hv = slice(h * HEAD_V, (h + 1) * HEAD_V)
            o = o_scr[:, hv] + ob_ref[:, hv]
            ms = jnp.mean(o * o, axis=-1, keepdims=True)
            y = o * lax.rsqrt(ms + EPS) * gn_ref[...]
            r = r_ref[:, hv].astype(F32)
            o_ref[:, hv] = (y * (r * jax.nn.sigmoid(r))).astype(o_ref.dtype)


def _gla_scan(main, lr, wdec, bdec, seq, reverse, o_bwd=None, g_norm=None):
    t = main.shape[0]
    bsz = t // seq
    bt = min(512, seq)
    nb = seq // bt

    def row(b, n):
        return b * nb + ((nb - 1 - n) if reverse else n)

    in_specs = [
        pl.BlockSpec((bt, GLA_KDIM), lambda b, n: (row(b, n), 0)),
        pl.BlockSpec((bt, GLA_KDIM), lambda b, n: (row(b, n), 1)),
        pl.BlockSpec((bt, GLA_VDIM), lambda b, n: (row(b, n), 1)),
        pl.BlockSpec((bt, LANES), lambda b, n: (row(b, n), 0)),
        pl.BlockSpec((LANES, GLA_KDIM), lambda b, n: (0, 0)),
        pl.BlockSpec((1, GLA_KDIM), lambda b, n: (0, 0)),
    ]
    args = [main, main, main, lr, wdec, bdec]
    scratch = [pltpu.VMEM((GLA_HEADS, HEAD_K, HEAD_V), F32)]
    if reverse:
        out_dtype = F32
    else:
        in_specs += [
            pl.BlockSpec((bt, GLA_VDIM), lambda b, n: (row(b, n), 0)),
            pl.BlockSpec((bt, GLA_VDIM), lambda b, n: (row(b, n), 2)),
            pl.BlockSpec((1, HEAD_V), lambda b, n: (0, 0)),
        ]
        args += [o_bwd, main, g_norm.reshape(1, HEAD_V)]
        scratch.append(pltpu.VMEM((bt, GLA_VDIM), F32))
        out_dtype = BF16
    return pl.pallas_call(
        functools.partial(_gla_kernel, reverse=reverse, bt=bt),
        grid=(bsz, nb),
        in_specs=in_specs,
        out_specs=pl.BlockSpec((bt, GLA_VDIM), lambda b, n: (row(b, n), 0)),
        out_shape=jax.ShapeDtypeStruct((t, GLA_VDIM), out_dtype),
        scratch_shapes=scratch,
        compiler_params=_cparams(("parallel", "arbitrary")),
        name="gla_bwd" if reverse else "gla_fwd",
    )(*args)


def _pool_tile(up_ref, uc_ref, un_ref, w_ref, b_ref, s_ref, ext_ref, *, tm, tiles_per_seq, seq):
    it = pl.program_id(0) % tiles_per_seq
    prev = up_ref[...].astype(F32)
    nxt = un_ref[...].astype(F32)
    ext_ref[0:POOL_HALO, :] = jnp.where(it == 0, 0.0, prev)
    ext_ref[POOL_HALO:POOL_HALO + tm, :] = uc_ref[...].astype(F32)
    ext_ref[POOL_HALO + tm:, :] = jnp.where(it == tiles_per_seq - 1, 0.0, nxt)
    pos = it * tm + lax.broadcasted_iota(jnp.int32, (tm, POOL_GROUP_DIM), 0)
    outs = []
    for gi, w in enumerate(POOL_WINDOWS):
        cs = slice(gi * POOL_GROUP_DIM, (gi + 1) * POOL_GROUP_DIM)
        start = POOL_HALO - w // 2
        acc = ext_ref[start:start + tm, cs]
        for j in range(1, w):
            acc = acc + ext_ref[start + j:start + j + tm, cs]
        cnt = jnp.minimum(pos + w // 2, seq) - jnp.maximum(pos - w // 2, 0)
        p = acc / cnt.astype(F32) - ext_ref[POOL_HALO:POOL_HALO + tm, cs]
        y = _dot(p.astype(BF16), w_ref[0, gi])
        outs.append(((y + b_ref[0, :, cs]) * s_ref[0, :, cs]).astype(BF16))
    return jnp.concatenate(outs, axis=1)


def _merge_kernel(a_ref, up_ref, uc_ref, un_ref, pw_ref, pb_ref, ps_ref, ga_ref, gb_ref, x_ref,
                  wa_ref, wb_ref, wo_ref, g1_ref, fg_ref, sc_ref, sh_ref, wr_ref, br_ref,
                  xo_ref, h_ref, rt_ref, ext_ref, *, tm, tiles_per_seq, seq):
    b_in = _pool_tile(up_ref, uc_ref, un_ref, pw_ref, pb_ref, ps_ref, ext_ref,
                      tm=tm, tiles_per_seq=tiles_per_seq, seq=seq)
    ya = _dot(a_ref[...], wa_ref[0])
    yb = _dot(b_in, wb_ref[0])
    merged = jax.nn.sigmoid(ga_ref[...].astype(F32)) * ya + jax.nn.sigmoid(gb_ref[...].astype(F32)) * yb
    xn = x_ref[...] + g1_ref[0] * _dot(merged.astype(BF16), wo_ref[0])
    xo_ref[...] = xn
    ms = jnp.mean(xn * xn, axis=-1, keepdims=True)
    h = xn * lax.rsqrt(ms + EPS) * fg_ref[...] * (1.0 + sc_ref[0]) + sh_ref[0]
    h_ref[...] = _pack_bf16_halves(h)

    hh, hl = _split2(h)
    both = _dot(hh, wr_ref[0])
    lg = both[:, :LANES] + (both[:, LANES:] + _dot(hl, wr_ref[0, :, :LANES])) + br_ref[0]
    lane = lax.broadcasted_iota(jnp.int32, lg.shape, 1).astype(F32)
    neg = -jnp.inf
    far = float(LANES)
    coarse = lane < N_GROUPS
    cmax = jnp.max(jnp.where(coarse, lg, neg), axis=-1, keepdims=True)
    g_star = jnp.min(jnp.where(coarse & (lg == cmax), lane, far), axis=-1, keepdims=True)
    p_g = 1.0 / jnp.sum(jnp.where(coarse, jnp.exp(lg - cmax), 0.0), axis=-1, keepdims=True)
    lo = N_GROUPS + EXPERTS_PER_GROUP * g_star
    fine = (lane >= lo) & (lane < lo + EXPERTS_PER_GROUP)
    v1 = jnp.max(jnp.where(fine, lg, neg), axis=-1, keepdims=True)
    i1 = jnp.min(jnp.where(fine & (lg == v1), lane, far), axis=-1, keepdims=True)
    fine2 = fine & (lane != i1)
    v2 = jnp.max(jnp.where(fine2, lg, neg), axis=-1, keepdims=True)
    i2 = jnp.min(jnp.where(fine2 & (lg == v2), lane, far), axis=-1, keepdims=True)
    e = jnp.exp(v2 - v1)
    w1 = p_g / (1.0 + e)
    w2 = p_g * e / (1.0 + e)
    rt = jnp.where(lane == 0, i1 - N_GROUPS,
                   jnp.where(lane == 1, i2 - N_GROUPS,
                             jnp.where(lane == 2, w1, jnp.where(lane == 3, w2, 0.0))))
    rt_ref[...] = rt


def _merge(a_in, main, x2, pool_w, pool_b, pool_s, wa, wb, wo, gate1, fg, scale2, shift2, wr, br,
           layer, seq):
    t, d = x2.shape
    tm = min(256, seq)
    tiles_per_seq = seq // tm
    const = dict(pipeline_mode=pl.Buffered(1))
    bidx = lambda i: (i // tiles_per_seq, 0, 0)
    hb = tm // POOL_HALO
    n_hblocks = t // POOL_HALO
    u_col = 3
    n_win = len(POOL_WINDOWS)
    return pl.pallas_call(
        functools.partial(_merge_kernel, tm=tm, tiles_per_seq=tiles_per_seq, seq=seq),
        grid=(t // tm,),
        in_specs=[
            pl.BlockSpec((tm, GLA_VDIM), lambda i: (i, 0)),
            pl.BlockSpec((POOL_HALO, POOL_DIM), lambda i: (jnp.maximum(i * hb - 1, 0), u_col)),
            pl.BlockSpec((tm, POOL_DIM), lambda i: (i, u_col)),
            pl.BlockSpec((POOL_HALO, POOL_DIM), lambda i: (jnp.minimum((i + 1) * hb, n_hblocks - 1), u_col)),
            pl.BlockSpec((1, n_win, POOL_GROUP_DIM, POOL_GROUP_DIM), lambda i: (layer, 0, 0, 0)),
            pl.BlockSpec((1, 1, POOL_DIM), lambda i: (layer, 0, 0)),
            pl.BlockSpec((1, 1, POOL_DIM), lambda i: (layer, 0, 0)),
            pl.BlockSpec((tm, d), lambda i: (i, 2)),
            pl.BlockSpec((tm, d), lambda i: (i, 3)),
            pl.BlockSpec((tm, d), lambda i: (i, 0)),
            pl.BlockSpec((1, GLA_VDIM, d), lambda i: (layer, 0, 0), **const),
            pl.BlockSpec((1, POOL_DIM, d), lambda i: (layer, 0, 0), **const),
            pl.BlockSpec((1, d, d), lambda i: (layer, 0, 0), **const),
            pl.BlockSpec((1, 1, d), bidx),
            pl.BlockSpec((1, d), lambda i: (0, 0)),
            pl.BlockSpec((1, 1, d), bidx),
            pl.BlockSpec((1, 1, d), bidx),
            pl.BlockSpec((1, d, 2 * LANES), lambda i: (layer, 0, 0), **const),
            pl.BlockSpec((1, 1, LANES), lambda i: (layer, 0, 0)),
        ],
        out_specs=[
            pl.BlockSpec((tm, d), lambda i: (i, 0)),
            pl.BlockSpec((tm, d // 2), lambda i: (i, 0)),
            pl.BlockSpec((tm, LANES), lambda i: (i, 0)),
        ],
        out_shape=[
            jax.ShapeDtypeStruct((t, d), F32),
            jax.ShapeDtypeStruct((t, d // 2), jnp.uint32),
            jax.ShapeDtypeStruct((t, LANES), F32),
        ],
        scratch_shapes=[pltpu.VMEM((tm + 2 * POOL_HALO, POOL_DIM), F32)],
        compiler_params=_cparams(("parallel",)),
        name="merge",
    )(a_in, main, main, main, pool_w, pool_b, pool_s, main, main, x2, wa, wb, wo, gate1, fg.reshape(1, d),
      scale2, shift2, wr, br)


def _moe_kernel(be_ref, nv_ref, j0_ref, src_ref, dst_ref, h_hbm, w1_ref, w3_ref, w2_ref, y_hbm,
                xbuf, ybuf, w1b, w3b, w2b, wview, gsem, ssem, *, n_blocks):
    b = pl.program_id(0)
    slot = b % 2

    def gather_copy(sl, src_row, g, u):
        return pltpu.make_async_copy(h_hbm.at[pl.ds(src_row, 1)], xbuf.at[sl, g, pl.ds(u, 1)], gsem.at[sl])

    def scatter_copy(sl, g, u, dst_row):
        return pltpu.make_async_copy(ybuf.at[sl, g, pl.ds(u, 1)], y_hbm.at[pl.ds(dst_row, 1)], ssem.at[sl])

    def issue_gather_group(base, sl, g):
        for u in range(ISSUE_UNROLL):
            gather_copy(sl, src_ref[base + g * ISSUE_UNROLL + u], g, u).start(priority=u % DMA_PRIORITIES)

    def issue_scatter(blk, sl):
        base = j0_ref[blk]
        n = nv_ref[blk]
        full = n // ISSUE_UNROLL

        def group(g, carry):
            for u in range(ISSUE_UNROLL):
                scatter_copy(sl, g, u, dst_ref[base + g * ISSUE_UNROLL + u]).start(priority=u % DMA_PRIORITIES)
            return carry
        lax.fori_loop(0, full, group, 0)

        def tail(i, carry):
            scatter_copy(sl, full, i - full * ISSUE_UNROLL, dst_ref[base + i]).start()
            return carry
        lax.fori_loop(full * ISSUE_UNROLL, n, tail, 0)

    def wait_rows(sem, n):
        pltpu.make_async_copy(wview.at[pl.ds(0, n)], wview.at[pl.ds(0, n)], sem).wait()

    def wait_gather(sl):
        wait_rows(gsem.at[sl], MOE_BLOCK)

    def wait_scatter(blk, sl):
        wait_rows(ssem.at[sl], nv_ref[blk])

    n_groups = MOE_BLOCK // ISSUE_UNROLL
    used = nv_ref[b] > 0
    prev_used = nv_ref[jnp.maximum(b - 1, 0)] > 0

    @pl.when(b == 0)
    def _():
        def first(g, carry):
            issue_gather_group(j0_ref[0], 0, g)
            return carry
        lax.fori_loop(0, n_groups, first, 0)

    @pl.when((b >= 2) & (nv_ref[jnp.maximum(b - 2, 0)] > 0))
    def _():
        wait_scatter(b - 2, slot)

    @pl.when(used)
    def _():
        prev_e = be_ref[jnp.maximum(b - 1, 0)]

        @pl.when((b == 0) | (be_ref[b] != prev_e))
        def _():
            w1b[...] = w1_ref[0, 0].astype(BF16)
            w3b[...] = w3_ref[0, 0].astype(BF16)
            w2b[...] = w2_ref[0, 0].astype(BF16)

        wait_gather(slot)
        next_base = j0_ref[jnp.minimum(b + 1, n_blocks - 1)]
        dh = xbuf.shape[-1]
        pieces = n_groups // 2
        kc = 2 * dh // pieces
        xb = _unpack_bf16_halves(xbuf[slot].reshape(MOE_BLOCK, dh)).astype(BF16)
        a = g = None
        for k in range(pieces):
            ks = slice(k * kc, (k + 1) * kc)
            pa = _dot(xb[:, ks], w1b[ks, :])
            pg = _dot(xb[:, ks], w3b[ks, :])
            a = pa if a is None else a + pa
            g = pg if g is None else g + pg
            issue_gather_group(next_base, 1 - slot, k)
        mid = (a * jax.nn.sigmoid(a) * g).astype(BF16)
        for n in range(pieces // 2):
            ns = slice(n * kc, (n + 1) * kc)
            ns_hi = slice(dh + n * kc, dh + (n + 1) * kc)
            packed = _pack_bf16_pair(_dot(mid, w2b[:, ns]), _dot(mid, w2b[:, ns_hi]))
            ybuf[slot, :, :, ns] = packed.reshape(n_groups, ISSUE_UNROLL, kc)
            issue_gather_group(next_base, 1 - slot, pieces + 2 * n)
            issue_gather_group(next_base, 1 - slot, pieces + 2 * n + 1)
        issue_scatter(b, slot)

    @pl.when(jnp.logical_not(used) & (b > 0) & prev_used)
    def _():
        wait_gather(slot)

    @pl.when(b == n_blocks - 1)
    def _():
        @pl.when(used)
        def _():
            wait_gather(1 - slot)

        @pl.when((b >= 1) & prev_used)
        def _():
            wait_scatter(b - 1, 1 - slot)

        @pl.when(used)
        def _():
            wait_scatter(b, slot)


def _moe_experts(h2, block_expert, n_valid, j0, src_rows, dst_rows, w1, w3, w2, layer):
    t, dh = h2.shape
    d = 2 * dh
    n_blocks = block_expert.shape[0]
    f = w1.shape[-1]
    row_buf = pltpu.VMEM((2, MOE_BLOCK // ISSUE_UNROLL, ISSUE_UNROLL, dh), jnp.uint32)
    grid_spec = pltpu.PrefetchScalarGridSpec(
        num_scalar_prefetch=5,
        grid=(n_blocks,),
        in_specs=[
            pl.BlockSpec(memory_space=pl.ANY),
            pl.BlockSpec((1, 1, d, f), lambda b, be, nv, j0, sr, ds: (layer, be[b], 0, 0)),
            pl.BlockSpec((1, 1, d, f), lambda b, be, nv, j0, sr, ds: (layer, be[b], 0, 0)),
            pl.BlockSpec((1, 1, f, d), lambda b, be, nv, j0, sr, ds: (layer, be[b], 0, 0)),
        ],
        out_specs=pl.BlockSpec(memory_space=pl.ANY),
        scratch_shapes=[
            row_buf,
            row_buf,
            pltpu.VMEM((d, f), BF16),
            pltpu.VMEM((d, f), BF16),
            pltpu.VMEM((f, d), BF16),
            pltpu.VMEM((MOE_BLOCK, dh // LANES, LANES), jnp.uint32),
            pltpu.SemaphoreType.DMA((2,)),
            pltpu.SemaphoreType.DMA((2,)),
        ],
    )
    return pl.pallas_call(
        functools.partial(_moe_kernel, n_blocks=n_blocks),
        grid_spec=grid_spec,
        out_shape=jax.ShapeDtypeStruct((TOP_K_FINE * t, dh), jnp.uint32),
        compiler_params=_cparams(("arbitrary",)),
        name="moe_experts",
    )(block_expert, n_valid, j0, src_rows, dst_rows, h2, w1, w3, w2)


def _route_plan(route, t):
    n_assign = t * TOP_K_FINE
    n_blocks = -(-n_assign // MOE_BLOCK) + N_EXPERTS
    e_flat = route[:, :TOP_K_FINE].astype(jnp.int32).reshape(-1)
    eids = jnp.arange(N_EXPERTS, dtype=jnp.int32)
    counts = jnp.sum((e_flat[:, None] == eids[None, :]).astype(jnp.int32), axis=0)
    start = jnp.cumsum(counts) - counts
    nblk = (counts + MOE_BLOCK - 1) // MOE_BLOCK
    bend = jnp.cumsum(nblk)
    bstart = bend - nblk
    order = jnp.argsort(e_flat, stable=True).astype(jnp.int32)
    blocks = jnp.arange(n_blocks, dtype=jnp.int32)[:, None]
    owner = ((blocks >= bstart[None, :]) & (blocks < bend[None, :])).astype(jnp.int32)
    block_expert = jnp.minimum(jnp.sum((blocks >= bend[None, :]).astype(jnp.int32), axis=1), N_EXPERTS - 1)
    in_expert = blocks[:, 0] - jnp.sum(owner * bstart[None, :], axis=1)
    n_valid = jnp.clip(jnp.sum(owner * counts[None, :], axis=1) - in_expert * MOE_BLOCK, 0, MOE_BLOCK)
    j0 = jnp.minimum(jnp.sum(owner * start[None, :], axis=1) + in_expert * MOE_BLOCK, n_assign)
    pad = jnp.zeros((MOE_BLOCK,), jnp.int32)
    src_rows = jnp.concatenate([order // TOP_K_FINE, pad])
    dst_rows = jnp.concatenate([(order % TOP_K_FINE) * t + order // TOP_K_FINE, pad])
    return block_expert, n_valid, j0, src_rows, dst_rows


def _combine_kernel(x_ref, y0_ref, y1_ref, rt_ref, g2_ref, fg_ref, o_ref, *, final):
    rt = rt_ref[...]
    y0 = _unpack_bf16_halves(y0_ref[...])
    y1 = _unpack_bf16_halves(y1_ref[...])
    xn = x_ref[...] + g2_ref[0] * (rt[:, 2:3] * y0 + rt[:, 3:4] * y1)
    if final:
        ms = jnp.mean(xn * xn, axis=-1, keepdims=True)
        xn = xn * lax.rsqrt(ms + EPS) * fg_ref[...]
    o_ref[...] = xn


def _combine(x2, y, route, gate2, final_g, seq, final):
    t, d = x2.shape
    tm = min(512, seq)
    tiles_per_seq = seq // tm
    n_tiles = t // tm
    return pl.pallas_call(
        functools.partial(_combine_kernel, final=final),
        grid=(n_tiles,),
        in_specs=[
            pl.BlockSpec((tm, d), lambda i: (i, 0)),
            pl.BlockSpec((tm, d // 2), lambda i: (i, 0)),
            pl.BlockSpec((tm, d // 2), lambda i: (i + n_tiles, 0)),
            pl.BlockSpec((tm, LANES), lambda i: (i, 0)),
            pl.BlockSpec((1, 1, d), lambda i: (i // tiles_per_seq, 0, 0)),
            pl.BlockSpec((1, d), lambda i: (0, 0)),
        ],
        out_specs=pl.BlockSpec((tm, d), lambda i: (i, 0)),
        out_shape=jax.ShapeDtypeStruct((t, d), F32),
        compiler_params=_cparams(("parallel",)),
        name="combine_final" if final else "combine",
    )(x2, y, y, route, gate2, final_g.reshape(1, d))


def _prep_in_w(in_w):
    lr0 = 2 * GLA_KDIM + 2 * GLA_VDIM
    lr1 = lr0 + 2 * DECAY_RANK
    w_main = jnp.concatenate([in_w[:, :, :lr0].astype(BF16), in_w[:, :, lr1:].astype(BF16)], axis=2)
    w_lr = jnp.pad(in_w[:, :, lr0:lr1].astype(BF16), ((0, 0), (0, 0), (0, LANES - 2 * DECAY_RANK)))
    return w_main, w_lr


def _prep_router(wc, bc, wf, bf):
    pad = LANES - N_GROUPS - N_EXPERTS
    wr = jnp.pad(jnp.concatenate([wc, wf], axis=2), ((0, 0), (0, 0), (0, pad)))
    wr_hi = wr.astype(BF16)
    wr_lo = (wr - wr_hi.astype(F32)).astype(BF16)
    br = jnp.pad(jnp.concatenate([bc, bf], axis=1), ((0, 0), (0, pad)))[:, None, :]
    return jnp.concatenate([wr_hi, wr_lo], axis=2), br


def kernel(x, c, ada_w, ada_b, mix_norm_g, in_w, decay_fw_w, decay_fw_b, decay_bw_w, decay_bw_b, gla_norm_g, pool_w, pool_b, pool_scale, branch_a_w, branch_b_w, out_w, ffn_norm_g, router_coarse_w, router_coarse_b, router_fine_w, router_fine_b, expert_w1, expert_w3, expert_w2, final_norm_g):
    bsz, seq, d = x.shape
    t = bsz * seq
    n_layers = ada_w.shape[0]
    mod = _ada_mod(c, ada_w, ada_b)
    w_main, w_lr = _prep_in_w(in_w)
    wr, br = _prep_router(router_coarse_w, router_coarse_b, router_fine_w, router_fine_b)
    wa, wb, wo = branch_a_w.astype(BF16), branch_b_w.astype(BF16), out_w.astype(BF16)
    rank_pad = ((0, 0), (0, LANES - 2 * DECAY_RANK), (0, 0))
    wdec_f = jnp.pad(jnp.concatenate([decay_fw_w, jnp.zeros_like(decay_fw_w)], axis=1), rank_pad)
    wdec_b = jnp.pad(jnp.concatenate([jnp.zeros_like(decay_bw_w), decay_bw_w], axis=1), rank_pad)
    pool_wb = pool_w.astype(BF16)
    pool_b3, pool_s3 = pool_b[:, None, :], pool_scale[:, None, :]

    x2 = x.reshape(t, d)
    for l in range(n_layers):
        shift1, scale1, gate1, shift2, scale2, gate2 = [mod[l, :, i] for i in range(N_MOD)]
        main, lr = _in_proj(x2, mix_norm_g[l], scale1, shift1, w_main, w_lr, l, seq)
        o_bwd = _gla_scan(main, lr, wdec_b[l], decay_bw_b[l].reshape(1, GLA_KDIM), seq, reverse=True)
        a_in = _gla_scan(main, lr, wdec_f[l], decay_fw_b[l].reshape(1, GLA_KDIM), seq, reverse=False,
                         o_bwd=o_bwd, g_norm=gla_norm_g[l])
        x2, h2, route = _merge(a_in, main, x2, pool_wb, pool_b3, pool_s3, wa, wb, wo, gate1, ffn_norm_g[l],
                               scale2, shift2, wr, br, l, seq)
        block_expert, n_valid, j0, src_rows, dst_rows = _route_plan(route, t)
        y = _moe_experts(h2, block_expert, n_valid, j0, src_rows, dst_rows, expert_w1, expert_w3, expert_w2, l)
        x2 = _combine(x2, y, route, gate2, final_norm_g, seq, final=(l == n_layers - 1))
    return x2.reshape(bsz, seq, d)
```

```python
import functools

import jax
import jax.numpy as jnp
from jax import lax
from jax.experimental import pallas as pl
from jax.experimental.pallas import tpu as pltpu

F32 = jnp.float32
BF16 = jnp.bfloat16

D_MODEL = 2048
GLA_HEADS = 4
GLA_VDIM = 1024
GLA_KDIM = 512
HEAD_K = 128
HEAD_V = 256
DECAY_RANK = 16
GATE_NORMALIZER = 16.0
LOG2_E = 1.4426950408889634
CHUNK = 128
CUMSUM_ROWS = 256
POOL_DIM = 1024
POOL_WINDOWS = (2, 4, 8, 16)
POOL_GROUP_DIM = 256
POOL_HALO = 16
N_GROUPS = 8
EXPERTS_PER_GROUP = 8
N_EXPERTS = 64
TOP_K_FINE = 2
D_FF_EXPERT = 256
MOE_BLOCK = 128
ISSUE_UNROLL = 8
GATHER_AHEAD = 2
DMA_PRIORITIES = 2
N_MOD = 6
EPS = 1e-6
LANES = 128
MAIN_COLS = 8192
VMEM_LIMIT = 56 * 1024 * 1024


def _cparams(sem):
    return pltpu.CompilerParams(dimension_semantics=sem, vmem_limit_bytes=VMEM_LIMIT)


def _split2(a):
    hi = a.astype(BF16)
    lo = (a - hi.astype(F32)).astype(BF16)
    return hi, lo


def _split3(a):
    p1 = a.astype(BF16)
    r1 = a - p1.astype(F32)
    p2 = r1.astype(BF16)
    p3 = (r1 - p2.astype(F32)).astype(BF16)
    return p1, p2, p3


def _dot(a, b):
    return jnp.dot(a, b, preferred_element_type=F32)


HIGH_HALF = 0xFFFF0000


def _pack_bf16_pair(lo, hi):
    lo = lax.bitcast_convert_type(lo.astype(BF16).astype(F32), jnp.uint32)
    hi = lax.bitcast_convert_type(hi.astype(BF16).astype(F32), jnp.uint32)
    return (lo >> 16) | (hi & jnp.uint32(HIGH_HALF))


def _pack_bf16_halves(v):
    n = v.shape[1] // 2
    return _pack_bf16_pair(v[:, :n], v[:, n:])


def _unpack_bf16_halves(w):
    lo = lax.bitcast_convert_type(w << 16, F32)
    hi = lax.bitcast_convert_type(w & jnp.uint32(HIGH_HALF), F32)
    return jnp.concatenate([lo, hi], axis=1)


def _dot3(a, b):
    ah, al = _split2(a)
    bh, bl = _split2(b)
    return _dot(ah, bh) + (_dot(ah, bl) + _dot(al, bh))


def _ada_kernel(c_ref, w_ref, b_ref, o_ref):
    c = c_ref[...]
    s = (c * jax.nn.sigmoid(c)).astype(BF16)
    o_ref[0] = _dot(s, w_ref[0].astype(BF16)) + b_ref[0]


def _ada_mod(c, ada_w, ada_b):
    n_layers, d, n = ada_w.shape
    bsz = c.shape[0]
    rows = 8
    tn = 2048
    c_pad = jnp.zeros((rows, d), F32).at[:bsz].set(c)
    out = pl.pallas_call(
        _ada_kernel,
        grid=(n_layers, n // tn),
        in_specs=[
            pl.BlockSpec((rows, d), lambda l, j: (0, 0)),
            pl.BlockSpec((1, d, tn), lambda l, j: (l, 0, j)),
            pl.BlockSpec((1, 1, tn), lambda l, j: (l, 0, j)),
        ],
        out_specs=pl.BlockSpec((1, rows, tn), lambda l, j: (l, 0, j)),
        out_shape=jax.ShapeDtypeStruct((n_layers, rows, n), F32),
        compiler_params=_cparams(("parallel", "parallel")),
        name="ada_mod",
    )(c_pad, ada_w, ada_b.reshape(n_layers, 1, n))
    return out[:, :bsz].reshape(n_layers, bsz, N_MOD, 1, d)


def _inproj_kernel(x_ref, g_ref, sc_ref, sh_ref, w_ref, wlr_ref, o_ref, lr_ref, h_scr):
    @pl.when(pl.program_id(1) == 0)
    def _():
        x = x_ref[...]
        ms = jnp.mean(x * x, axis=-1, keepdims=True)
        y = x * lax.rsqrt(ms + EPS) * g_ref[...]
        hb = (y * (1.0 + sc_ref[0]) + sh_ref[0]).astype(BF16)
        h_scr[...] = hb
        lr_ref[...] = _dot(hb, wlr_ref[0])

    o_ref[...] = _dot(h_scr[...], w_ref[0]).astype(o_ref.dtype)


def _in_proj(x2, g, scale, shift, w_main, w_lr, layer, seq):
    t, d = x2.shape
    tm = min(1024, seq)
    tn = 1024
    tiles_per_seq = seq // tm
    bidx = lambda i, j: (i // tiles_per_seq, 0, 0)
    return pl.pallas_call(
        _inproj_kernel,
        grid=(t // tm, MAIN_COLS // tn),
        in_specs=[
            pl.BlockSpec((tm, d), lambda i, j: (i, 0)),
            pl.BlockSpec((1, d), lambda i, j: (0, 0)),
            pl.BlockSpec((1, 1, d), bidx),
            pl.BlockSpec((1, 1, d), bidx),
            pl.BlockSpec((1, d, tn), lambda i, j: (layer, 0, j)),
            pl.BlockSpec((1, d, LANES), lambda i, j: (layer, 0, 0)),
        ],
        out_specs=[
            pl.BlockSpec((tm, tn), lambda i, j: (i, j)),
            pl.BlockSpec((tm, LANES), lambda i, j: (i, 0)),
        ],
        out_shape=[
            jax.ShapeDtypeStruct((t, MAIN_COLS), BF16),
            jax.ShapeDtypeStruct((t, LANES), F32),
        ],
        scratch_shapes=[pltpu.VMEM((tm, d), BF16)],
        compiler_params=_cparams(("parallel", "arbitrary")),
        name="in_proj",
    )(x2, g.reshape(1, d), scale, shift, w_main, w_lr)


def _gla_kernel(*refs, reverse, bt):
    if reverse:
        q_ref, k_ref, v_ref, lr_ref, wd_ref, bd_ref, o_ref, st_ref = refs
    else:
        (q_ref, k_ref, v_ref, lr_ref, wd_ref, bd_ref, ob_ref, r_ref, gn_ref,
         o_ref, st_ref, o_scr) = refs

    @pl.when(pl.program_id(1) == 0)
    def _():
        st_ref[...] = jnp.zeros_like(st_ref)

    z = _dot3(lr_ref[...], wd_ref[...]) + bd_ref[...]
    la = (jnp.minimum(z, 0.0) - jnp.log(1.0 + jnp.exp(-jnp.abs(z)))) * (LOG2_E / GATE_NORMALIZER)

    tb = min(bt, CUMSUM_ROWS)
    rows = lax.broadcasted_iota(jnp.int32, (tb, tb), 0)
    cols = lax.broadcasted_iota(jnp.int32, (tb, tb), 1)
    same_chunk = (rows // CHUNK) == (cols // CHUNK)
    ordered = (cols >= rows) if reverse else (cols <= rows)
    tri = jnp.where(same_chunk & ordered, 1.0, 0.0).astype(BF16)
    l1, l2, l3 = _split3(la)
    cum = jnp.concatenate(
        [_dot(tri, l1[s:s + tb]) + (_dot(tri, l2[s:s + tb]) + _dot(tri, l3[s:s + tb])) for s in range(0, bt, tb)],
        axis=0)

    crow = lax.broadcasted_iota(jnp.int32, (CHUNK, CHUNK), 0)
    ccol = lax.broadcasted_iota(jnp.int32, (CHUNK, CHUNK), 1)
    att_mask = (ccol > crow) if reverse else (ccol <= crow)

    n_chunks = bt // CHUNK
    chunk_order = range(n_chunks - 1, -1, -1) if reverse else range(n_chunks)
    q_scale = HEAD_K ** -0.5
    for c in chunk_order:
        sl = slice(c * CHUNK, (c + 1) * CHUNK)
        for h in range(GLA_HEADS):
            hk = slice(h * HEAD_K, (h + 1) * HEAD_K)
            hv = slice(h * HEAD_V, (h + 1) * HEAD_V)
            state = st_ref[h]
            cm = cum[sl, hk]
            last = cm[0:1] if reverse else cm[CHUNK - 1:CHUNK]
            mid = cm[CHUNK // 2:CHUNK // 2 + 1]
            qf = q_ref[sl, hk].astype(F32) * q_scale
            kf = k_ref[sl, hk].astype(F32)
            vb = v_ref[sl, hv]
            q_in = (qf * jnp.exp2(cm)).astype(BF16)
            q_att = (qf * jnp.exp2(cm - mid)).astype(BF16)
            k_att = (kf * jnp.exp2(mid - cm)).astype(BF16)
            k_end = (kf * jnp.exp2(last - cm)).astype(BF16)
            att = lax.dot_general(q_att, k_att, (((1,), (1,)), ((), ())), preferred_element_type=F32)
            att = jnp.where(att_mask, att, 0.0).astype(BF16)
            o = _dot(att, vb) + _dot(q_in, state.astype(BF16))
            if reverse:
                o_ref[sl, hv] = o
            else:
                o_scr[sl, hv] = o
            dec_col = jnp.transpose(jnp.broadcast_to(jnp.exp2(last), (HEAD_K, HEAD_K)))
            dec = jnp.concatenate([dec_col, dec_col], axis=1)
            kv = lax.dot_general(k_end, vb, (((0,), (0,)), ((), ())), preferred_element_type=F32)
            st_ref[h] = dec * state + kv

    if not reverse:
        for h in range(GLA_HEADS):
            hv = slice(h * HEAD_V, (h + 1) * HEAD_V)
            o = o_scr[:, hv] + ob_ref[:, hv]
            ms = jnp.mean(o * o, axis=-1, keepdims=True)
            y = o * lax.rsqrt(ms + EPS) * gn_ref[...]
            r = r_ref[:, hv].astype(F32)
            o_ref[:, hv] = (y * (r * jax.nn.sigmoid(r))).astype(o_ref.dtype)


def _gla_scan(main, lr, wdec, bdec, seq, reverse, o_bwd=None, g_norm=None):
    t = main.shape[0]
    bsz = t // seq
    bt = min(512, seq)
    nb = seq // bt

    def row(b, n):
        return b * nb + ((nb - 1 - n) if reverse else n)

    in_specs = [
        pl.BlockSpec((bt, GLA_KDIM), lambda b, n: (row(b, n), 0)),
        pl.BlockSpec((bt, GLA_KDIM), lambda b, n: (row(b, n), 1)),
        pl.BlockSpec((bt, GLA_VDIM), lambda b, n: (row(b, n), 1)),
        pl.BlockSpec((bt, LANES), lambda b, n: (row(b, n), 0)),
        pl.BlockSpec((LANES, GLA_KDIM), lambda b, n: (0, 0)),
        pl.BlockSpec((1, GLA_KDIM), lambda b, n: (0, 0)),
    ]
    args = [main, main, main, lr, wdec, bdec]
    scratch = [pltpu.VMEM((GLA_HEADS, HEAD_K, HEAD_V), F32)]
    if reverse:
        out_dtype = F32
    else:
        in_specs += [
            pl.BlockSpec((bt, GLA_VDIM), lambda b, n: (row(b, n), 0)),
            pl.BlockSpec((bt, GLA_VDIM), lambda b, n: (row(b, n), 2)),
            pl.BlockSpec((1, HEAD_V), lambda b, n: (0, 0)),
        ]
        args += [o_bwd, main, g_norm.reshape(1, HEAD_V)]
        scratch.append(pltpu.VMEM((bt, GLA_VDIM), F32))
        out_dtype = BF16
    return pl.pallas_call(
        functools.partial(_gla_kernel, reverse=reverse, bt=bt),
        grid=(bsz, nb),
        in_specs=in_specs,
        out_specs=pl.BlockSpec((bt, GLA_VDIM), lambda b, n: (row(b, n), 0)),
        out_shape=jax.ShapeDtypeStruct((t, GLA_VDIM), out_dtype),
        scratch_shapes=scratch,
        compiler_params=_cparams(("parallel", "arbitrary")),
        name="gla_bwd" if reverse else "gla_fwd",
    )(*args)


def _pool_tile(up_ref, uc_ref, un_ref, w_ref, b_ref, s_ref, ext_ref, *, tm, tiles_per_seq, seq):
    it = pl.program_id(0) % tiles_per_seq
    prev = up_ref[...].astype(F32)
    nxt = un_ref[...].astype(F32)
    ext_ref[0:POOL_HALO, :] = jnp.where(it == 0, 0.0, prev)
    ext_ref[POOL_HALO:POOL_HALO + tm, :] = uc_ref[...].astype(F32)
    ext_ref[POOL_HALO + tm:, :] = jnp.where(it == tiles_per_seq - 1, 0.0, nxt)
    pos = it * tm + lax.broadcasted_iota(jnp.int32, (tm, POOL_GROUP_DIM), 0)
    outs = []
    for gi, w in enumerate(POOL_WINDOWS):
        cs = slice(gi * POOL_GROUP_DIM, (gi + 1) * POOL_GROUP_DIM)
        start = POOL_HALO - w // 2
        acc = ext_ref[start:start + tm, cs]
        for j in range(1, w):
            acc = acc + ext_ref[start + j:start + j + tm, cs]
        cnt = jnp.minimum(pos + w // 2, seq) - jnp.maximum(pos - w // 2, 0)
        p = acc / cnt.astype(F32) - ext_ref[POOL_HALO:POOL_HALO + tm, cs]
        y = _dot(p.astype(BF16), w_ref[0, gi])
        outs.append(((y + b_ref[0, :, cs]) * s_ref[0, :, cs]).astype(BF16))
    return jnp.concatenate(outs, axis=1)


def _merge_kernel(a_ref, up_ref, uc_ref, un_ref, pw_ref, pb_ref, ps_ref, ga_ref, gb_ref, x_ref,
                  wa_ref, wb_ref, wo_ref, g1_ref, fg_ref, sc_ref, sh_ref, wr_ref, br_ref,
                  xo_ref, h_ref, rt_ref, ext_ref, *, tm, tiles_per_seq, seq):
    b_in = _pool_tile(up_ref, uc_ref, un_ref, pw_ref, pb_ref, ps_ref, ext_ref,
                      tm=tm, tiles_per_seq=tiles_per_seq, seq=seq)
    ya = _dot(a_ref[...], wa_ref[0])
    yb = _dot(b_in, wb_ref[0])
    merged = jax.nn.sigmoid(ga_ref[...].astype(F32)) * ya + jax.nn.sigmoid(gb_ref[...].astype(F32)) * yb
    xn = x_ref[...] + g1_ref[0] * _dot(merged.astype(BF16), wo_ref[0])
    xo_ref[...] = xn
    ms = jnp.mean(xn * xn, axis=-1, keepdims=True)
    h = xn * lax.rsqrt(ms + EPS) * fg_ref[...] * (1.0 + sc_ref[0]) + sh_ref[0]
    h_ref[...] = _pack_bf16_halves(h)

    hh, hl = _split2(h)
    both = _dot(hh, wr_ref[0])
    lg = both[:, :LANES] + (both[:, LANES:] + _dot(hl, wr_ref[0, :, :LANES])) + br_ref[0]
    lane = lax.broadcasted_iota(jnp.int32, lg.shape, 1).astype(F32)
    neg = -jnp.inf
    far = float(LANES)
    coarse = lane < N_GROUPS
    cmax = jnp.max(jnp.where(coarse, lg, neg), axis=-1, keepdims=True)
    g_star = jnp.min(jnp.where(coarse & (lg == cmax), lane, far), axis=-1, keepdims=True)
    p_g = 1.0 / jnp.sum(jnp.where(coarse, jnp.exp(lg - cmax), 0.0), axis=-1, keepdims=True)
    lo = N_GROUPS + EXPERTS_PER_GROUP * g_star
    fine = (lane >= lo) & (lane < lo + EXPERTS_PER_GROUP)
    v1 = jnp.max(jnp.where(fine, lg, neg), axis=-1, keepdims=True)
    i1 = jnp.min(jnp.where(fine & (lg == v1), lane, far), axis=-1, keepdims=True)
    fine2 = fine & (lane != i1)
    v2 = jnp.max(jnp.where(fine2, lg, neg), axis=-1, keepdims=True)
    i2 = jnp.min(jnp.where(fine2 & (lg == v2), lane, far), axis=-1, keepdims=True)
    e = jnp.exp(v2 - v1)
    w1 = p_g / (1.0 + e)
    w2 = p_g * e / (1.0 + e)
    rt = jnp.where(lane == 0, i1 - N_GROUPS,
                   jnp.where(lane == 1, i2 - N_GROUPS,
                             jnp.where(lane == 2, w1, jnp.where(lane == 3, w2, 0.0))))
    rt_ref[...] = rt


def _merge(a_in, main, x2, pool_w, pool_b, pool_s, wa, wb, wo, gate1, fg, scale2, shift2, wr, br,
           layer, seq):
    t, d = x2.shape
    tm = min(256, seq)
    tiles_per_seq = seq // tm
    const = dict(pipeline_mode=pl.Buffered(1))
    bidx = lambda i: (i // tiles_per_seq, 0, 0)
    hb = tm // POOL_HALO
    n_hblocks = t // POOL_HALO
    u_col = 3
    n_win = len(POOL_WINDOWS)
    return pl.pallas_call(
        functools.partial(_merge_kernel, tm=tm, tiles_per_seq=tiles_per_seq, seq=seq),
        grid=(t // tm,),
        in_specs=[
            pl.BlockSpec((tm, GLA_VDIM), lambda i: (i, 0)),
            pl.BlockSpec((POOL_HALO, POOL_DIM), lambda i: (jnp.maximum(i * hb - 1, 0), u_col)),
            pl.BlockSpec((tm, POOL_DIM), lambda i: (i, u_col)),
            pl.BlockSpec((POOL_HALO, POOL_DIM), lambda i: (jnp.minimum((i + 1) * hb, n_hblocks - 1), u_col)),
            pl.BlockSpec((1, n_win, POOL_GROUP_DIM, POOL_GROUP_DIM), lambda i: (layer, 0, 0, 0)),
            pl.BlockSpec((1, 1, POOL_DIM), lambda i: (layer, 0, 0)),
            pl.BlockSpec((1, 1, POOL_DIM), lambda i: (layer, 0, 0)),
            pl.BlockSpec((tm, d), lambda i: (i, 2)),
            pl.BlockSpec((tm, d), lambda i: (i, 3)),
            pl.BlockSpec((tm, d), lambda i: (i, 0)),
            pl.BlockSpec((1, GLA_VDIM, d), lambda i: (layer, 0, 0), **const),
            pl.BlockSpec((1, POOL_DIM, d), lambda i: (layer, 0, 0), **const),
            pl.BlockSpec((1, d, d), lambda i: (layer, 0, 0), **const),
            pl.BlockSpec((1, 1, d), bidx),
            pl.BlockSpec((1, d), lambda i: (0, 0)),
            pl.BlockSpec((1, 1, d), bidx),
            pl.BlockSpec((1, 1, d), bidx),
            pl.BlockSpec((1, d, 2 * LANES), lambda i: (layer, 0, 0), **const),
            pl.BlockSpec((1, 1, LANES), lambda i: (layer, 0, 0)),
        ],
        out_specs=[
            pl.BlockSpec((tm, d), lambda i: (i, 0)),
            pl.BlockSpec((tm, d // 2), lambda i: (i, 0)),
            pl.BlockSpec((tm, LANES), lambda i: (i, 0)),
        ],
        out_shape=[
            jax.ShapeDtypeStruct((t, d), F32),
            jax.ShapeDtypeStruct((t, d // 2), jnp.uint32),
            jax.ShapeDtypeStruct((t, LANES), F32),
        ],
        scratch_shapes=[pltpu.VMEM((tm + 2 * POOL_HALO, POOL_DIM), F32)],
        compiler_params=_cparams(("parallel",)),
        name="merge",
    )(a_in, main, main, main, pool_w, pool_b, pool_s, main, main, x2, wa, wb, wo, gate1, fg.reshape(1, d),
      scale2, shift2, wr, br)


def _moe_kernel(be_ref, nv_ref, j0_ref, src_ref, dst_ref, h_hbm, w1_ref, w3_ref, w2_ref, y_hbm,
                xbuf, ybuf, w1b, w3b, w2b, wview, gsem, ssem, *, n_blocks):
    b = pl.program_id(0)
    slot = b % 2

    def gather_copy(sl, src_row, g, u):
        return pltpu.make_async_copy(h_hbm.at[pl.ds(src_row, 1)], xbuf.at[sl, g, pl.ds(u, 1)], gsem.at[sl])

    def scatter_copy(sl, g, u, dst_row):
        return pltpu.make_async_copy(ybuf.at[sl, g, pl.ds(u, 1)], y_hbm.at[pl.ds(dst_row, 1)], ssem.at[sl])

    def issue_gather_group(base, sl, g):
        for u in range(ISSUE_UNROLL):
            gather_copy(sl, src_ref[base + g * ISSUE_UNROLL + u], g, u).start(priority=u % DMA_PRIORITIES)

    def issue_scatter(blk, sl):
        base = j0_ref[blk]
        n = nv_ref[blk]
        full = n // ISSUE_UNROLL

        def group(g, carry):
            for u in range(ISSUE_UNROLL):
                scatter_copy(sl, g, u, dst_ref[base + g * ISSUE_UNROLL + u]).start(priority=u % DMA_PRIORITIES)
            return carry
        lax.fori_loop(0, full, group, 0)

        def tail(i, carry):
            scatter_copy(sl, full, i - full * ISSUE_UNROLL, dst_ref[base + i]).start()
            return carry
        lax.fori_loop(full * ISSUE_UNROLL, n, tail, 0)

    def wait_rows(sem, n):
        pltpu.make_async_copy(wview.at[pl.ds(0, n)], wview.at[pl.ds(0, n)], sem).wait()

    def wait_gather(sl):
        wait_rows(gsem.at[sl], MOE_BLOCK)

    def wait_scatter(blk, sl):
        wait_rows(ssem.at[sl], nv_ref[blk])

    n_groups = MOE_BLOCK // ISSUE_UNROLL
    used = nv_ref[b] > 0
    prev_used = nv_ref[jnp.maximum(b - 1, 0)] > 0
    prev2_used = nv_ref[jnp.maximum(b - 2, 0)] > 0
    xslot = b % (GATHER_AHEAD + 1)
    ahead_slot = (b + GATHER_AHEAD) % (GATHER_AHEAD + 1)

    @pl.when(b == 0)
    def _():
        for blk in range(GATHER_AHEAD):
            def first(g, carry, blk=blk):
                issue_gather_group(j0_ref[blk], blk, g)
                return carry
            lax.fori_loop(0, n_groups, first, 0)

    @pl.when((b >= 2) & prev2_used)
    def _():
        wait_scatter(b - 2, slot)

    @pl.when(used)
    def _():
        prev_e = be_ref[jnp.maximum(b - 1, 0)]

        @pl.when((b == 0) | (be_ref[b] != prev_e))
        def _():
            w1b[...] = w1_ref[0, 0].astype(BF16)
            w3b[...] = w3_ref[0, 0].astype(BF16)
            w2b[...] = w2_ref[0, 0].astype(BF16)

        wait_gather(xslot)
        next_base = j0_ref[jnp.minimum(b + GATHER_AHEAD, n_blocks - 1)]
        dh = xbuf.shape[-1]
        pieces = n_groups // 2
        kc = 2 * dh // pieces
        xb = _unpack_bf16_halves(xbuf[xslot].reshape(MOE_BLOCK, dh)).astype(BF16)
        a = g = None
        for k in range(pieces):
            ks = slice(k * kc, (k + 1) * kc)
            pa = _dot(xb[:, ks], w1b[ks, :])
            pg = _dot(xb[:, ks], w3b[ks, :])
            a = pa if a is None else a + pa
            g = pg if g is None else g + pg
            issue_gather_group(next_base, ahead_slot, k)
        mid = (a * jax.nn.sigmoid(a) * g).astype(BF16)
        for n in range(pieces // 2):
            ns = slice(n * kc, (n + 1) * kc)
            ns_hi = slice(dh + n * kc, dh + (n + 1) * kc)
            packed = _pack_bf16_pair(_dot(mid, w2b[:, ns]), _dot(mid, w2b[:, ns_hi]))
            ybuf[slot, :, :, ns] = packed.reshape(n_groups, ISSUE_UNROLL, kc)
            issue_gather_group(next_base, ahead_slot, pieces + 2 * n)
            issue_gather_group(next_base, ahead_slot, pieces + 2 * n + 1)
        issue_scatter(b, slot)

    @pl.when(jnp.logical_not(used) & ((b < GATHER_AHEAD) | prev2_used))
    def _():
        wait_gather(xslot)

    @pl.when(b == n_blocks - 1)
    def _():
        @pl.when(prev_used)
        def _():
            wait_gather((b + 1) % (GATHER_AHEAD + 1))

        @pl.when(used)
        def _():
            wait_gather(ahead_slot)

        @pl.when((b >= 1) & prev_used)
        def _():
            wait_scatter(b - 1, 1 - slot)

        @pl.when(used)
        def _():
            wait_scatter(b, slot)


def _moe_experts(h2, block_expert, n_valid, j0, src_rows, dst_rows, w1, w3, w2, layer):
    t, dh = h2.shape
    d = 2 * dh
    n_blocks = block_expert.shape[0]
    f = w1.shape[-1]
    row_buf = lambda n: pltpu.VMEM((n, MOE_BLOCK // ISSUE_UNROLL, ISSUE_UNROLL, dh), jnp.uint32)
    grid_spec = pltpu.PrefetchScalarGridSpec(
        num_scalar_prefetch=5,
        grid=(n_blocks,),
        in_specs=[
            pl.BlockSpec(memory_space=pl.ANY),
            pl.BlockSpec((1, 1, d, f), lambda b, be, nv, j0, sr, ds: (layer, be[b], 0, 0)),
            pl.BlockSpec((1, 1, d, f), lambda b, be, nv, j0, sr, ds: (layer, be[b], 0, 0)),
            pl.BlockSpec((1, 1, f, d), lambda b, be, nv, j0, sr, ds: (layer, be[b], 0, 0)),
        ],
        out_specs=pl.BlockSpec(memory_space=pl.ANY),
        scratch_shapes=[
            row_buf(GATHER_AHEAD + 1),
            row_buf(2),
            pltpu.VMEM((d, f), BF16),
            pltpu.VMEM((d, f), BF16),
            pltpu.VMEM((f, d), BF16),
            pltpu.VMEM((MOE_BLOCK, dh // LANES, LANES), jnp.uint32),
            pltpu.SemaphoreType.DMA((GATHER_AHEAD + 1,)),
            pltpu.SemaphoreType.DMA((2,)),
        ],
    )
    return pl.pallas_call(
        functools.partial(_moe_kernel, n_blocks=n_blocks),
        grid_spec=grid_spec,
        out_shape=jax.ShapeDtypeStruct((TOP_K_FINE * t, dh), jnp.uint32),
        compiler_params=_cparams(("arbitrary",)),
        name="moe_experts",
    )(block_expert, n_valid, j0, src_rows, dst_rows, h2, w1, w3, w2)


def _route_plan(route, t):
    n_assign = t * TOP_K_FINE
    n_blocks = -(-n_assign // MOE_BLOCK) + N_EXPERTS
    e_flat = route[:, :TOP_K_FINE].astype(jnp.int32).reshape(-1)
    eids = jnp.arange(N_EXPERTS, dtype=jnp.int32)
    counts = jnp.sum((e_flat[:, None] == eids[None, :]).astype(jnp.int32), axis=0)
    start = jnp.cumsum(counts) - counts
    nblk = (counts + MOE_BLOCK - 1) // MOE_BLOCK
    bend = jnp.cumsum(nblk)
    bstart = bend - nblk
    order = jnp.argsort(e_flat, stable=True).astype(jnp.int32)
    blocks = jnp.arange(n_blocks, dtype=jnp.int32)[:, None]
    owner = ((blocks >= bstart[None, :]) & (blocks < bend[None, :])).astype(jnp.int32)
    block_expert = jnp.minimum(jnp.sum((blocks >= bend[None, :]).astype(jnp.int32), axis=1), N_EXPERTS - 1)
    in_expert = blocks[:, 0] - jnp.sum(owner * bstart[None, :], axis=1)
    n_valid = jnp.clip(jnp.sum(owner * counts[None, :], axis=1) - in_expert * MOE_BLOCK, 0, MOE_BLOCK)
    j0 = jnp.minimum(jnp.sum(owner * start[None, :], axis=1) + in_expert * MOE_BLOCK, n_assign)
    pad = jnp.zeros((MOE_BLOCK,), jnp.int32)
    src_rows = jnp.concatenate([order // TOP_K_FINE, pad])
    dst_rows = jnp.concatenate([(order % TOP_K_FINE) * t + order // TOP_K_FINE, pad])
    return block_expert, n_valid, j0, src_rows, dst_rows


def _combine_kernel(x_ref, y0_ref, y1_ref, rt_ref, g2_ref, fg_ref, o_ref, *, final):
    rt = rt_ref[...]
    y0 = _unpack_bf16_halves(y0_ref[...])
    y1 = _unpack_bf16_halves(y1_ref[...])
    xn = x_ref[...] + g2_ref[0] * (rt[:, 2:3] * y0 + rt[:, 3:4] * y1)
    if final:
        ms = jnp.mean(xn * xn, axis=-1, keepdims=True)
        xn = xn * lax.rsqrt(ms + EPS) * fg_ref[...]
    o_ref[...] = xn


def _combine(x2, y, route, gate2, final_g, seq, final):
    t, d = x2.shape
    tm = min(512, seq)
    tiles_per_seq = seq // tm
    n_tiles = t // tm
    return pl.pallas_call(
        functools.partial(_combine_kernel, final=final),
        grid=(n_tiles,),
        in_specs=[
            pl.BlockSpec((tm, d), lambda i: (i, 0)),
            pl.BlockSpec((tm, d // 2), lambda i: (i, 0)),
            pl.BlockSpec((tm, d // 2), lambda i: (i + n_tiles, 0)),
            pl.BlockSpec((tm, LANES), lambda i: (i, 0)),
            pl.BlockSpec((1, 1, d), lambda i: (i // tiles_per_seq, 0, 0)),
            pl.BlockSpec((1, d), lambda i: (0, 0)),
        ],
        out_specs=pl.BlockSpec((tm, d), lambda i: (i, 0)),
        out_shape=jax.ShapeDtypeStruct((t, d), F32),
        compiler_params=_cparams(("parallel",)),
        name="combine_final" if final else "combine",
    )(x2, y, y, route, gate2, final_g.reshape(1, d))


def _prep_in_w(in_w):
    lr0 = 2 * GLA_KDIM + 2 * GLA_VDIM
    lr1 = lr0 + 2 * DECAY_RANK
    w_main = jnp.concatenate([in_w[:, :, :lr0].astype(BF16), in_w[:, :, lr1:].astype(BF16)], axis=2)
    w_lr = jnp.pad(in_w[:, :, lr0:lr1].astype(BF16), ((0, 0), (0, 0), (0, LANES - 2 * DECAY_RANK)))
    return w_main, w_lr


def _prep_router(wc, bc, wf, bf):
    pad = LANES - N_GROUPS - N_EXPERTS
    wr = jnp.pad(jnp.concatenate([wc, wf], axis=2), ((0, 0), (0, 0), (0, pad)))
    wr_hi = wr.astype(BF16)
    wr_lo = (wr - wr_hi.astype(F32)).astype(BF16)
    br = jnp.pad(jnp.concatenate([bc, bf], axis=1), ((0, 0), (0, pad)))[:, None, :]
    return jnp.concatenate([wr_hi, wr_lo], axis=2), br


def kernel(x, c, ada_w, ada_b, mix_norm_g, in_w, decay_fw_w, decay_fw_b, decay_bw_w, decay_bw_b, gla_norm_g, pool_w, pool_b, pool_scale, branch_a_w, branch_b_w, out_w, ffn_norm_g, router_coarse_w, router_coarse_b, router_fine_w, router_fine_b, expert_w1, expert_w3, expert_w2, final_norm_g):
    bsz, seq, d = x.shape
    t = bsz * seq
    n_layers = ada_w.shape[0]
    mod = _ada_mod(c, ada_w, ada_b)
    w_main, w_lr = _prep_in_w(in_w)
    wr, br = _prep_router(router_coarse_w, router_coarse_b, router_fine_w, router_fine_b)
    wa, wb, wo = branch_a_w.astype(BF16), branch_b_w.astype(BF16), out_w.astype(BF16)
    rank_pad = ((0, 0), (0, LANES - 2 * DECAY_RANK), (0, 0))
    wdec_f = jnp.pad(jnp.concatenate([decay_fw_w, jnp.zeros_like(decay_fw_w)], axis=1), rank_pad)
    wdec_b = jnp.pad(jnp.concatenate([jnp.zeros_like(decay_bw_w), decay_bw_w], axis=1), rank_pad)
    pool_wb = pool_w.astype(BF16)
    pool_b3, pool_s3 = pool_b[:, None, :], pool_scale[:, None, :]

    x2 = x.reshape(t, d)
    for l in range(n_layers):
        shift1, scale1, gate1, shift2, scale2, gate2 = [mod[l, :, i] for i in range(N_MOD)]
        main, lr = _in_proj(x2, mix_norm_g[l], scale1, shift1, w_main, w_lr, l, seq)
        o_bwd = _gla_scan(main, lr, wdec_b[l], decay_bw_b[l].reshape(1, GLA_KDIM), seq, reverse=True)
        a_in = _gla_scan(main, lr, wdec_f[l], decay_fw_b[l].reshape(1, GLA_KDIM), seq, reverse=False,
                         o_bwd=o_bwd, g_norm=gla_norm_g[l])
        x2, h2, route = _merge(a_in, main, x2, pool_wb, pool_b3, pool_s3, wa, wb, wo, gate1, ffn_norm_g[l],
                               scale2, shift2, wr, br, l, seq)
        block_expert, n_valid, j0, src_rows, dst_rows = _route_plan(route, t)
        y = _moe_experts(h2, block_expert, n_valid, j0, src_rows, dst_rows, expert_w1, expert_w3, expert_w2, l)
        x2 = _combine(x2, y, route, gate2, final_norm_g, seq, final=(l == n_layers - 1))
    return x2.reshape(bsz, seq, d)
```

```python
import functools

import jax
import jax.numpy as jnp
from jax import lax
from jax.experimental import pallas as pl
from jax.experimental.pallas import tpu as pltpu

F32 = jnp.float32
BF16 = jnp.bfloat16

D_MODEL = 2048
GLA_HEADS = 4
GLA_VDIM = 1024
GLA_KDIM = 512
HEAD_K = 128
HEAD_V = 256
DECAY_RANK = 16
GATE_NORMALIZER = 16.0
LOG2_E = 1.4426950408889634
CHUNK = 128
CUMSUM_ROWS = 256
POOL_DIM = 1024
POOL_WINDOWS = (2, 4, 8, 16)
POOL_GROUP_DIM = 256
POOL_HALO = 16
N_GROUPS = 8
EXPERTS_PER_GROUP = 8
N_EXPERTS = 64
TOP_K_FINE = 2
D_FF_EXPERT = 256
MOE_BLOCK = 128
ISSUE_UNROLL = 8
GATHER_AHEAD = 2
DMA_PRIORITIES = 2
N_MOD = 6
EPS = 1e-6
LANES = 128
MAIN_COLS = 8192
VMEM_LIMIT = 56 * 1024 * 1024


def _cparams(sem):
    return pltpu.CompilerParams(dimension_semantics=sem, vmem_limit_bytes=VMEM_LIMIT)


def _split2(a):
    hi = a.astype(BF16)
    lo = (a - hi.astype(F32)).astype(BF16)
    return hi, lo


def _split3(a):
    p1 = a.astype(BF16)
    r1 = a - p1.astype(F32)
    p2 = r1.astype(BF16)
    p3 = (r1 - p2.astype(F32)).astype(BF16)
    return p1, p2, p3


def _dot(a, b):
    return jnp.dot(a, b, preferred_element_type=F32)


HIGH_HALF = 0xFFFF0000


def _pack_bf16_pair(lo, hi):
    lo = lax.bitcast_convert_type(lo.astype(BF16).astype(F32), jnp.uint32)
    hi = lax.bitcast_convert_type(hi.astype(BF16).astype(F32), jnp.uint32)
    return (lo >> 16) | (hi & jnp.uint32(HIGH_HALF))


def _pack_bf16_halves(v):
    n = v.shape[1] // 2
    return _pack_bf16_pair(v[:, :n], v[:, n:])


def _unpack_bf16_halves(w):
    lo = lax.bitcast_convert_type(w << 16, F32)
    hi = lax.bitcast_convert_type(w & jnp.uint32(HIGH_HALF), F32)
    return jnp.concatenate([lo, hi], axis=1)


def _store_slabs(ref, words):
    for c in range(ref.shape[-2]):
        ref[:, c, :] = words[:, c * LANES:(c + 1) * LANES]


def _load_slabs(ref):
    return jnp.concatenate([ref[:, c, :] for c in range(ref.shape[-2])], axis=1)


def _dot3(a, b):
    ah, al = _split2(a)
    bh, bl = _split2(b)
    return _dot(ah, bh) + (_dot(ah, bl) + _dot(al, bh))


def _ada_kernel(c_ref, w_ref, b_ref, o_ref):
    c = c_ref[...]
    s = (c * jax.nn.sigmoid(c)).astype(BF16)
    o_ref[0] = _dot(s, w_ref[0].astype(BF16)) + b_ref[0]


def _ada_mod(c, ada_w, ada_b):
    n_layers, d, n = ada_w.shape
    bsz = c.shape[0]
    rows = 8
    tn = 2048
    c_pad = jnp.zeros((rows, d), F32).at[:bsz].set(c)
    out = pl.pallas_call(
        _ada_kernel,
        grid=(n_layers, n // tn),
        in_specs=[
            pl.BlockSpec((rows, d), lambda l, j: (0, 0)),
            pl.BlockSpec((1, d, tn), lambda l, j: (l, 0, j)),
            pl.BlockSpec((1, 1, tn), lambda l, j: (l, 0, j)),
        ],
        out_specs=pl.BlockSpec((1, rows, tn), lambda l, j: (l, 0, j)),
        out_shape=jax.ShapeDtypeStruct((n_layers, rows, n), F32),
        compiler_params=_cparams(("parallel", "parallel")),
        name="ada_mod",
    )(c_pad, ada_w, ada_b.reshape(n_layers, 1, n))
    return out[:, :bsz].reshape(n_layers, bsz, N_MOD, 1, d)


def _inproj_kernel(x_ref, g_ref, sc_ref, sh_ref, w_ref, wlr_ref, o_ref, lr_ref, h_scr):
    @pl.when(pl.program_id(1) == 0)
    def _():
        x = x_ref[...]
        ms = jnp.mean(x * x, axis=-1, keepdims=True)
        y = x * lax.rsqrt(ms + EPS) * g_ref[...]
        hb = (y * (1.0 + sc_ref[0]) + sh_ref[0]).astype(BF16)
        h_scr[...] = hb
        lr_ref[...] = _dot(hb, wlr_ref[0])

    o_ref[...] = _dot(h_scr[...], w_ref[0]).astype(o_ref.dtype)


def _in_proj(x2, g, scale, shift, w_main, w_lr, layer, seq):
    t, d = x2.shape
    tm = min(1024, seq)
    tn = 1024
    tiles_per_seq = seq // tm
    bidx = lambda i, j: (i // tiles_per_seq, 0, 0)
    return pl.pallas_call(
        _inproj_kernel,
        grid=(t // tm, MAIN_COLS // tn),
        in_specs=[
            pl.BlockSpec((tm, d), lambda i, j: (i, 0)),
            pl.BlockSpec((1, d), lambda i, j: (0, 0)),
            pl.BlockSpec((1, 1, d), bidx),
            pl.BlockSpec((1, 1, d), bidx),
            pl.BlockSpec((1, d, tn), lambda i, j: (layer, 0, j)),
            pl.BlockSpec((1, d, LANES), lambda i, j: (layer, 0, 0)),
        ],
        out_specs=[
            pl.BlockSpec((tm, tn), lambda i, j: (i, j)),
            pl.BlockSpec((tm, LANES), lambda i, j: (i, 0)),
        ],
        out_shape=[
            jax.ShapeDtypeStruct((t, MAIN_COLS), BF16),
            jax.ShapeDtypeStruct((t, LANES), F32),
        ],
        scratch_shapes=[pltpu.VMEM((tm, d), BF16)],
        compiler_params=_cparams(("parallel", "arbitrary")),
        name="in_proj",
    )(x2, g.reshape(1, d), scale, shift, w_main, w_lr)


def _gla_kernel(*refs, reverse, bt):
    if reverse:
        q_ref, k_ref, v_ref, lr_ref, wd_ref, bd_ref, o_ref, st_ref = refs
    else:
        (q_ref, k_ref, v_ref, lr_ref, wd_ref, bd_ref, ob_ref, r_ref, gn_ref,
         o_ref, st_ref, o_scr) = refs

    @pl.when(pl.program_id(1) == 0)
    def _():
        st_ref[...] = jnp.zeros_like(st_ref)

    z = _dot3(lr_ref[...], wd_ref[...]) + bd_ref[...]
    la = (jnp.minimum(z, 0.0) - jnp.log(1.0 + jnp.exp(-jnp.abs(z)))) * (LOG2_E / GATE_NORMALIZER)

    tb = min(bt, CUMSUM_ROWS)
    rows = lax.broadcasted_iota(jnp.int32, (tb, tb), 0)
    cols = lax.broadcasted_iota(jnp.int32, (tb, tb), 1)
    same_chunk = (rows // CHUNK) == (cols // CHUNK)
    ordered = (cols >= rows) if reverse else (cols <= rows)
    tri = jnp.where(same_chunk & ordered, 1.0, 0.0).astype(BF16)
    l1, l2, l3 = _split3(la)
    cum = jnp.concatenate(
        [_dot(tri, l1[s:s + tb]) + (_dot(tri, l2[s:s + tb]) + _dot(tri, l3[s:s + tb])) for s in range(0, bt, tb)],
        axis=0)

    crow = lax.broadcasted_iota(jnp.int32, (CHUNK, CHUNK), 0)
    ccol = lax.broadcasted_iota(jnp.int32, (CHUNK, CHUNK), 1)
    att_mask = (ccol > crow) if reverse else (ccol <= crow)

    n_chunks = bt // CHUNK
    chunk_order = range(n_chunks - 1, -1, -1) if reverse else range(n_chunks)
    q_scale = HEAD_K ** -0.5
    for c in chunk_order:
        sl = slice(c * CHUNK, (c + 1) * CHUNK)
        for h in range(GLA_HEADS):
            hk = slice(h * HEAD_K, (h + 1) * HEAD_K)
            hv = slice(h * HEAD_V, (h + 1) * HEAD_V)
            state = st_ref[h]
            cm = cum[sl, hk]
            last = cm[0:1] if reverse else cm[CHUNK - 1:CHUNK]
            mid = cm[CHUNK // 2:CHUNK // 2 + 1]
            qf = q_ref[sl, hk].astype(F32) * q_scale
            kf = k_ref[sl, hk].astype(F32)
            vb = v_ref[sl, hv]
            q_in = (qf * jnp.exp2(cm)).astype(BF16)
            q_att = (qf * jnp.exp2(cm - mid)).astype(BF16)
            k_att = (kf * jnp.exp2(mid - cm)).astype(BF16)
            k_end = (kf * jnp.exp2(last - cm)).astype(BF16)
            att = lax.dot_general(q_att, k_att, (((1,), (1,)), ((), ())), preferred_element_type=F32)
            att = jnp.where(att_mask, att, 0.0).astype(BF16)
            o = _dot(att, vb) + _dot(q_in, state.astype(BF16))
            if reverse:
                o_ref[sl, hv] = o
            else:
                o_scr[sl, hv] = o
            dec_col = jnp.transpose(jnp.broadcast_to(jnp.exp2(last), (HEAD_K, HEAD_K)))
            dec = jnp.concatenate([dec_col, dec_col], axis=1)
            kv = lax.dot_general(k_end, vb, (((0,), (0,)), ((), ())), preferred_element_type=F32)
            st_ref[h] = dec * state + kv

    if not reverse:
        for h in range(GLA_HEADS):
            hv = slice(h * HEAD_V, (h + 1) * HEAD_V)
            o = o_scr[:, hv] + ob_ref[:, hv]
            ms = jnp.mean(o * o, axis=-1, keepdims=True)
            y = o * lax.rsqrt(ms + EPS) * gn_ref[...]
            r = r_ref[:, hv].astype(F32)
            o_ref[:, hv] = (y * (r * jax.nn.sigmoid(r))).astype(o_ref.dtype)


def _gla_scan(main, lr, wdec, bdec, seq, reverse, o_bwd=None, g_norm=None):
    t = main.shape[0]
    bsz = t // seq
    bt = min(512, seq)
    nb = seq // bt

    def row(b, n):
        return b * nb + ((nb - 1 - n) if reverse else n)

    in_specs = [
        pl.BlockSpec((bt, GLA_KDIM), lambda b, n: (row(b, n), 0)),
        pl.BlockSpec((bt, GLA_KDIM), lambda b, n: (row(b, n), 1)),
        pl.BlockSpec((bt, GLA_VDIM), lambda b, n: (row(b, n), 1)),
        pl.BlockSpec((bt, LANES), lambda b, n: (row(b, n), 0)),
        pl.BlockSpec((LANES, GLA_KDIM), lambda b, n: (0, 0)),
        pl.BlockSpec((1, GLA_KDIM), lambda b, n: (0, 0)),
    ]
    args = [main, main, main, lr, wdec, bdec]
    scratch = [pltpu.VMEM((GLA_HEADS, HEAD_K, HEAD_V), F32)]
    if reverse:
        out_dtype = F32
    else:
        in_specs += [
            pl.BlockSpec((bt, GLA_VDIM), lambda b, n: (row(b, n), 0)),
            pl.BlockSpec((bt, GLA_VDIM), lambda b, n: (row(b, n), 2)),
            pl.BlockSpec((1, HEAD_V), lambda b, n: (0, 0)),
        ]
        args += [o_bwd, main, g_norm.reshape(1, HEAD_V)]
        scratch.append(pltpu.VMEM((bt, GLA_VDIM), F32))
        out_dtype = BF16
    return pl.pallas_call(
        functools.partial(_gla_kernel, reverse=reverse, bt=bt),
        grid=(bsz, nb),
        in_specs=in_specs,
        out_specs=pl.BlockSpec((bt, GLA_VDIM), lambda b, n: (row(b, n), 0)),
        out_shape=jax.ShapeDtypeStruct((t, GLA_VDIM), out_dtype),
        scratch_shapes=scratch,
        compiler_params=_cparams(("parallel", "arbitrary")),
        name="gla_bwd" if reverse else "gla_fwd",
    )(*args)


def _pool_tile(up_ref, uc_ref, un_ref, w_ref, b_ref, s_ref, ext_ref, *, tm, tiles_per_seq, seq):
    it = pl.program_id(0) % tiles_per_seq
    prev = up_ref[...].astype(F32)
    nxt = un_ref[...].astype(F32)
    ext_ref[0:POOL_HALO, :] = jnp.where(it == 0, 0.0, prev)
    ext_ref[POOL_HALO:POOL_HALO + tm, :] = uc_ref[...].astype(F32)
    ext_ref[POOL_HALO + tm:, :] = jnp.where(it == tiles_per_seq - 1, 0.0, nxt)
    pos = it * tm + lax.broadcasted_iota(jnp.int32, (tm, POOL_GROUP_DIM), 0)
    outs = []
    for gi, w in enumerate(POOL_WINDOWS):
        cs = slice(gi * POOL_GROUP_DIM, (gi + 1) * POOL_GROUP_DIM)
        start = POOL_HALO - w // 2
        acc = ext_ref[start:start + tm, cs]
        for j in range(1, w):
            acc = acc + ext_ref[start + j:start + j + tm, cs]
        cnt = jnp.minimum(pos + w // 2, seq) - jnp.maximum(pos - w // 2, 0)
        p = acc / cnt.astype(F32) - ext_ref[POOL_HALO:POOL_HALO + tm, cs]
        y = _dot(p.astype(BF16), w_ref[0, gi])
        outs.append(((y + b_ref[0, :, cs]) * s_ref[0, :, cs]).astype(BF16))
    return jnp.concatenate(outs, axis=1)


def _merge_kernel(a_ref, up_ref, uc_ref, un_ref, pw_ref, pb_ref, ps_ref, ga_ref, gb_ref, x_ref,
                  wa_ref, wb_ref, wo_ref, g1_ref, fg_ref, sc_ref, sh_ref, wr_ref, br_ref,
                  xo_ref, h_ref, rt_ref, ext_ref, *, tm, tiles_per_seq, seq):
    b_in = _pool_tile(up_ref, uc_ref, un_ref, pw_ref, pb_ref, ps_ref, ext_ref,
                      tm=tm, tiles_per_seq=tiles_per_seq, seq=seq)
    ya = _dot(a_ref[...], wa_ref[0])
    yb = _dot(b_in, wb_ref[0])
    merged = jax.nn.sigmoid(ga_ref[...].astype(F32)) * ya + jax.nn.sigmoid(gb_ref[...].astype(F32)) * yb
    xn = x_ref[...] + g1_ref[0] * _dot(merged.astype(BF16), wo_ref[0])
    xo_ref[...] = xn
    ms = jnp.mean(xn * xn, axis=-1, keepdims=True)
    h = xn * lax.rsqrt(ms + EPS) * fg_ref[...] * (1.0 + sc_ref[0]) + sh_ref[0]
    _store_slabs(h_ref, _pack_bf16_halves(h))

    hh, hl = _split2(h)
    both = _dot(hh, wr_ref[0])
    lg = both[:, :LANES] + (both[:, LANES:] + _dot(hl, wr_ref[0, :, :LANES])) + br_ref[0]
    lane = lax.broadcasted_iota(jnp.int32, lg.shape, 1).astype(F32)
    neg = -jnp.inf
    far = float(LANES)
    coarse = lane < N_GROUPS
    cmax = jnp.max(jnp.where(coarse, lg, neg), axis=-1, keepdims=True)
    g_star = jnp.min(jnp.where(coarse & (lg == cmax), lane, far), axis=-1, keepdims=True)
    p_g = 1.0 / jnp.sum(jnp.where(coarse, jnp.exp(lg - cmax), 0.0), axis=-1, keepdims=True)
    lo = N_GROUPS + EXPERTS_PER_GROUP * g_star
    fine = (lane >= lo) & (lane < lo + EXPERTS_PER_GROUP)
    v1 = jnp.max(jnp.where(fine, lg, neg), axis=-1, keepdims=True)
    i1 = jnp.min(jnp.where(fine & (lg == v1), lane, far), axis=-1, keepdims=True)
    fine2 = fine & (lane != i1)
    v2 = jnp.max(jnp.where(fine2, lg, neg), axis=-1, keepdims=True)
    i2 = jnp.min(jnp.where(fine2 & (lg == v2), lane, far), axis=-1, keepdims=True)
    e = jnp.exp(v2 - v1)
    w1 = p_g / (1.0 + e)
    w2 = p_g * e / (1.0 + e)
    rt = jnp.where(lane == 0, i1 - N_GROUPS,
                   jnp.where(lane == 1, i2 - N_GROUPS,
                             jnp.where(lane == 2, w1, jnp.where(lane == 3, w2, 0.0))))
    rt_ref[...] = rt


def _merge(a_in, main, x2, pool_w, pool_b, pool_s, wa, wb, wo, gate1, fg, scale2, shift2, wr, br,
           layer, seq):
    t, d = x2.shape
    tm = min(256, seq)
    tiles_per_seq = seq // tm
    const = dict(pipeline_mode=pl.Buffered(1))
    bidx = lambda i: (i // tiles_per_seq, 0, 0)
    hb = tm // POOL_HALO
    n_hblocks = t // POOL_HALO
    u_col = 3
    n_win = len(POOL_WINDOWS)
    return pl.pallas_call(
        functools.partial(_merge_kernel, tm=tm, tiles_per_seq=tiles_per_seq, seq=seq),
        grid=(t // tm,),
        in_specs=[
            pl.BlockSpec((tm, GLA_VDIM), lambda i: (i, 0)),
            pl.BlockSpec((POOL_HALO, POOL_DIM), lambda i: (jnp.maximum(i * hb - 1, 0), u_col)),
            pl.BlockSpec((tm, POOL_DIM), lambda i: (i, u_col)),
            pl.BlockSpec((POOL_HALO, POOL_DIM), lambda i: (jnp.minimum((i + 1) * hb, n_hblocks - 1), u_col)),
            pl.BlockSpec((1, n_win, POOL_GROUP_DIM, POOL_GROUP_DIM), lambda i: (layer, 0, 0, 0)),
            pl.BlockSpec((1, 1, POOL_DIM), lambda i: (layer, 0, 0)),
            pl.BlockSpec((1, 1, POOL_DIM), lambda i: (layer, 0, 0)),
            pl.BlockSpec((tm, d), lambda i: (i, 2)),
            pl.BlockSpec((tm, d), lambda i: (i, 3)),
            pl.BlockSpec((tm, d), lambda i: (i, 0)),
            pl.BlockSpec((1, GLA_VDIM, d), lambda i: (layer, 0, 0), **const),
            pl.BlockSpec((1, POOL_DIM, d), lambda i: (layer, 0, 0), **const),
            pl.BlockSpec((1, d, d), lambda i: (layer, 0, 0), **const),
            pl.BlockSpec((1, 1, d), bidx),
            pl.BlockSpec((1, d), lambda i: (0, 0)),
            pl.BlockSpec((1, 1, d), bidx),
            pl.BlockSpec((1, 1, d), bidx),
            pl.BlockSpec((1, d, 2 * LANES), lambda i: (layer, 0, 0), **const),
            pl.BlockSpec((1, 1, LANES), lambda i: (layer, 0, 0)),
        ],
        out_specs=[
            pl.BlockSpec((tm, d), lambda i: (i, 0)),
            pl.BlockSpec((tm, d // 2 // LANES, LANES), lambda i: (i, 0, 0)),
            pl.BlockSpec((tm, LANES), lambda i: (i, 0)),
        ],
        out_shape=[
            jax.ShapeDtypeStruct((t, d), F32),
            jax.ShapeDtypeStruct((t, d // 2 // LANES, LANES), jnp.uint32),
            jax.ShapeDtypeStruct((t, LANES), F32),
        ],
        scratch_shapes=[pltpu.VMEM((tm + 2 * POOL_HALO, POOL_DIM), F32)],
        compiler_params=_cparams(("parallel",)),
        name="merge",
    )(a_in, main, main, main, pool_w, pool_b, pool_s, main, main, x2, wa, wb, wo, gate1, fg.reshape(1, d),
      scale2, shift2, wr, br)


def _moe_kernel(be_ref, nv_ref, j0_ref, src_ref, dst_ref, h_hbm, w1_ref, w3_ref, w2_ref, y_hbm,
                xbuf, ybuf, w1b, w3b, w2b, gsem, ssem, *, n_blocks):
    b = pl.program_id(0)
    slot = b % 2

    def gather_copy(sl, src_rows, buf_rows):
        return pltpu.make_async_copy(h_hbm.at[src_rows], xbuf.at[sl, buf_rows], gsem.at[sl])

    def scatter_copy(sl, buf_rows, dst_rows):
        return pltpu.make_async_copy(ybuf.at[sl, buf_rows], y_hbm.at[dst_rows], ssem.at[sl])

    def issue_gather_group(base, sl, g):
        for u in range(ISSUE_UNROLL):
            i = g * ISSUE_UNROLL + u
            gather_copy(sl, src_ref[base + i], i).start(priority=u % DMA_PRIORITIES)

    def issue_scatter(blk, sl):
        base = j0_ref[blk]
        n = nv_ref[blk]
        full = n // ISSUE_UNROLL

        def group(g, carry):
            for u in range(ISSUE_UNROLL):
                i = g * ISSUE_UNROLL + u
                scatter_copy(sl, i, dst_ref[base + i]).start(priority=u % DMA_PRIORITIES)
            return carry
        lax.fori_loop(0, full, group, 0)

        def tail(i, carry):
            scatter_copy(sl, i, dst_ref[base + i]).start()
            return carry
        lax.fori_loop(full * ISSUE_UNROLL, n, tail, 0)

    def wait_gather(sl):
        gather_copy(sl, pl.ds(0, MOE_BLOCK), pl.ds(0, MOE_BLOCK)).wait()

    def wait_scatter(blk, sl):
        n = nv_ref[blk]
        scatter_copy(sl, pl.ds(0, n), pl.ds(0, n)).wait()

    n_groups = MOE_BLOCK // ISSUE_UNROLL
    used = nv_ref[b] > 0
    prev_used = nv_ref[jnp.maximum(b - 1, 0)] > 0
    prev2_used = nv_ref[jnp.maximum(b - 2, 0)] > 0
    xslot = b % (GATHER_AHEAD + 1)
    ahead_slot = (b + GATHER_AHEAD) % (GATHER_AHEAD + 1)

    @pl.when(b == 0)
    def _():
        for blk in range(GATHER_AHEAD):
            def first(g, carry, blk=blk):
                issue_gather_group(j0_ref[blk], blk, g)
                return carry
            lax.fori_loop(0, n_groups, first, 0)

    @pl.when((b >= 2) & prev2_used)
    def _():
        wait_scatter(b - 2, slot)

    @pl.when(used)
    def _():
        prev_e = be_ref[jnp.maximum(b - 1, 0)]

        @pl.when((b == 0) | (be_ref[b] != prev_e))
        def _():
            w1b[...] = w1_ref[0, 0].astype(BF16)
            w3b[...] = w3_ref[0, 0].astype(BF16)
            w2b[...] = w2_ref[0, 0].astype(BF16)

        wait_gather(xslot)
        next_base = j0_ref[jnp.minimum(b + GATHER_AHEAD, n_blocks - 1)]
        dh = xbuf.shape[-2] * LANES
        pieces = n_groups // 2
        kc = 2 * dh // pieces
        xb = _unpack_bf16_halves(_load_slabs(xbuf.at[xslot])).astype(BF16)
        a = g = None
        for k in range(pieces):
            ks = slice(k * kc, (k + 1) * kc)
            pa = _dot(xb[:, ks], w1b[ks, :])
            pg = _dot(xb[:, ks], w3b[ks, :])
            a = pa if a is None else a + pa
            g = pg if g is None else g + pg
            issue_gather_group(next_base, ahead_slot, k)
        mid = (a * jax.nn.sigmoid(a) * g).astype(BF16)
        for n in range(pieces // 2):
            ns = slice(n * kc, (n + 1) * kc)
            ns_hi = slice(dh + n * kc, dh + (n + 1) * kc)
            packed = _pack_bf16_pair(_dot(mid, w2b[:, ns]), _dot(mid, w2b[:, ns_hi]))
            for c in range(kc // LANES):
                ybuf[slot, :, n * (kc // LANES) + c, :] = packed[:, c * LANES:(c + 1) * LANES]
            issue_gather_group(next_base, ahead_slot, pieces + 2 * n)
            issue_gather_group(next_base, ahead_slot, pieces + 2 * n + 1)
        issue_scatter(b, slot)

    @pl.when(jnp.logical_not(used) & ((b < GATHER_AHEAD) | prev2_used))
    def _():
        wait_gather(xslot)

    @pl.when(b == n_blocks - 1)
    def _():
        @pl.when(prev_used)
        def _():
            wait_gather((b + 1) % (GATHER_AHEAD + 1))

        @pl.when(used)
        def _():
            wait_gather(ahead_slot)

        @pl.when((b >= 1) & prev_used)
        def _():
            wait_scatter(b - 1, 1 - slot)

        @pl.when(used)
        def _():
            wait_scatter(b, slot)


def _moe_experts(h2, block_expert, n_valid, j0, src_rows, dst_rows, w1, w3, w2, layer):
    t, n_slab, _ = h2.shape
    d = 2 * n_slab * LANES
    n_blocks = block_expert.shape[0]
    f = w1.shape[-1]
    row_buf = lambda n: pltpu.VMEM((n, MOE_BLOCK, n_slab, LANES), jnp.uint32)
    grid_spec = pltpu.PrefetchScalarGridSpec(
        num_scalar_prefetch=5,
        grid=(n_blocks,),
        in_specs=[
            pl.BlockSpec(memory_space=pl.ANY),
            pl.BlockSpec((1, 1, d, f), lambda b, be, nv, j0, sr, ds: (layer, be[b], 0, 0)),
            pl.BlockSpec((1, 1, d, f), lambda b, be, nv, j0, sr, ds: (layer, be[b], 0, 0)),
            pl.BlockSpec((1, 1, f, d), lambda b, be, nv, j0, sr, ds: (layer, be[b], 0, 0)),
        ],
        out_specs=pl.BlockSpec(memory_space=pl.ANY),
        scratch_shapes=[
            row_buf(GATHER_AHEAD + 1),
            row_buf(2),
            pltpu.VMEM((d, f), BF16),
            pltpu.VMEM((d, f), BF16),
            pltpu.VMEM((f, d), BF16),
            pltpu.SemaphoreType.DMA((GATHER_AHEAD + 1,)),
            pltpu.SemaphoreType.DMA((2,)),
        ],
    )
    return pl.pallas_call(
        functools.partial(_moe_kernel, n_blocks=n_blocks),
        grid_spec=grid_spec,
        out_shape=jax.ShapeDtypeStruct((TOP_K_FINE * t, n_slab, LANES), jnp.uint32),
        compiler_params=_cparams(("arbitrary",)),
        name="moe_experts",
    )(block_expert, n_valid, j0, src_rows, dst_rows, h2, w1, w3, w2)


def _route_plan(route, t):
    n_assign = t * TOP_K_FINE
    n_blocks = -(-n_assign // MOE_BLOCK) + N_EXPERTS
    e_flat = route[:, :TOP_K_FINE].astype(jnp.int32).reshape(-1)
    eids = jnp.arange(N_EXPERTS, dtype=jnp.int32)
    counts = jnp.sum((e_flat[:, None] == eids[None, :]).astype(jnp.int32), axis=0)
    start = jnp.cumsum(counts) - counts
    nblk = (counts + MOE_BLOCK - 1) // MOE_BLOCK
    bend = jnp.cumsum(nblk)
    bstart = bend - nblk
    order = jnp.argsort(e_flat, stable=True).astype(jnp.int32)
    blocks = jnp.arange(n_blocks, dtype=jnp.int32)[:, None]
    owner = ((blocks >= bstart[None, :]) & (blocks < bend[None, :])).astype(jnp.int32)
    block_expert = jnp.minimum(jnp.sum((blocks >= bend[None, :]).astype(jnp.int32), axis=1), N_EXPERTS - 1)
    in_expert = blocks[:, 0] - jnp.sum(owner * bstart[None, :], axis=1)
    n_valid = jnp.clip(jnp.sum(owner * counts[None, :], axis=1) - in_expert * MOE_BLOCK, 0, MOE_BLOCK)
    j0 = jnp.minimum(jnp.sum(owner * start[None, :], axis=1) + in_expert * MOE_BLOCK, n_assign)
    pad = jnp.zeros((MOE_BLOCK,), jnp.int32)
    src_rows = jnp.concatenate([order // TOP_K_FINE, pad])
    dst_rows = jnp.concatenate([(order % TOP_K_FINE) * t + order // TOP_K_FINE, pad])
    return block_expert, n_valid, j0, src_rows, dst_rows


def _combine_kernel(x_ref, y0_ref, y1_ref, rt_ref, g2_ref, fg_ref, o_ref, *, final):
    rt = rt_ref[...]
    y0 = _unpack_bf16_halves(_load_slabs(y0_ref))
    y1 = _unpack_bf16_halves(_load_slabs(y1_ref))
    xn = x_ref[...] + g2_ref[0] * (rt[:, 2:3] * y0 + rt[:, 3:4] * y1)
    if final:
        ms = jnp.mean(xn * xn, axis=-1, keepdims=True)
        xn = xn * lax.rsqrt(ms + EPS) * fg_ref[...]
    o_ref[...] = xn


def _combine(x2, y, route, gate2, final_g, seq, final):
    t, d = x2.shape
    tm = min(512, seq)
    tiles_per_seq = seq // tm
    n_tiles = t // tm
    return pl.pallas_call(
        functools.partial(_combine_kernel, final=final),
        grid=(n_tiles,),
        in_specs=[
            pl.BlockSpec((tm, d), lambda i: (i, 0)),
            pl.BlockSpec((tm, d // 2 // LANES, LANES), lambda i: (i, 0, 0)),
            pl.BlockSpec((tm, d // 2 // LANES, LANES), lambda i: (i + n_tiles, 0, 0)),
            pl.BlockSpec((tm, LANES), lambda i: (i, 0)),
            pl.BlockSpec((1, 1, d), lambda i: (i // tiles_per_seq, 0, 0)),
            pl.BlockSpec((1, d), lambda i: (0, 0)),
        ],
        out_specs=pl.BlockSpec((tm, d), lambda i: (i, 0)),
        out_shape=jax.ShapeDtypeStruct((t, d), F32),
        compiler_params=_cparams(("parallel",)),
        name="combine_final" if final else "combine",
    )(x2, y, y, route, gate2, final_g.reshape(1, d))


def _prep_in_w(in_w):
    lr0 = 2 * GLA_KDIM + 2 * GLA_VDIM
    lr1 = lr0 + 2 * DECAY_RANK
    w_main = jnp.concatenate([in_w[:, :, :lr0].astype(BF16), in_w[:, :, lr1:].astype(BF16)], axis=2)
    w_lr = jnp.pad(in_w[:, :, lr0:lr1].astype(BF16), ((0, 0), (0, 0), (0, LANES - 2 * DECAY_RANK)))
    return w_main, w_lr


def _prep_router(wc, bc, wf, bf):
    pad = LANES - N_GROUPS - N_EXPERTS
    wr = jnp.pad(jnp.concatenate([wc, wf], axis=2), ((0, 0), (0, 0), (0, pad)))
    wr_hi = wr.astype(BF16)
    wr_lo = (wr - wr_hi.astype(F32)).astype(BF16)
    br = jnp.pad(jnp.concatenate([bc, bf], axis=1), ((0, 0), (0, pad)))[:, None, :]
    return jnp.concatenate([wr_hi, wr_lo], axis=2), br


def kernel(x, c, ada_w, ada_b, mix_norm_g, in_w, decay_fw_w, decay_fw_b, decay_bw_w, decay_bw_b, gla_norm_g, pool_w, pool_b, pool_scale, branch_a_w, branch_b_w, out_w, ffn_norm_g, router_coarse_w, router_coarse_b, router_fine_w, router_fine_b, expert_w1, expert_w3, expert_w2, final_norm_g):
    bsz, seq, d = x.shape
    t = bsz * seq
    n_layers = ada_w.shape[0]
    mod = _ada_mod(c, ada_w, ada_b)
    w_main, w_lr = _prep_in_w(in_w)
    wr, br = _prep_router(router_coarse_w, router_coarse_b, router_fine_w, router_fine_b)
    wa, wb, wo = branch_a_w.astype(BF16), branch_b_w.astype(BF16), out_w.astype(BF16)
    rank_pad = ((0, 0), (0, LANES - 2 * DECAY_RANK), (0, 0))
    wdec_f = jnp.pad(jnp.concatenate([decay_fw_w, jnp.zeros_like(decay_fw_w)], axis=1), rank_pad)
    wdec_b = jnp.pad(jnp.concatenate([jnp.zeros_like(decay_bw_w), decay_bw_w], axis=1), rank_pad)
    pool_wb = pool_w.astype(BF16)
    pool_b3, pool_s3 = pool_b[:, None, :], pool_scale[:, None, :]

    x2 = x.reshape(t, d)
    for l in range(n_layers):
        shift1, scale1, gate1, shift2, scale2, gate2 = [mod[l, :, i] for i in range(N_MOD)]
        main, lr = _in_proj(x2, mix_norm_g[l], scale1, shift1, w_main, w_lr, l, seq)
        o_bwd = _gla_scan(main, lr, wdec_b[l], decay_bw_b[l].reshape(1, GLA_KDIM), seq, reverse=True)
        a_in = _gla_scan(main, lr, wdec_f[l], decay_fw_b[l].reshape(1, GLA_KDIM), seq, reverse=False,
                         o_bwd=o_bwd, g_norm=gla_norm_g[l])
        x2, h2, route = _merge(a_in, main, x2, pool_wb, pool_b3, pool_s3, wa, wb, wo, gate1, ffn_norm_g[l],
                               scale2, shift2, wr, br, l, seq)
        block_expert, n_valid, j0, src_rows, dst_rows = _route_plan(route, t)
        y = _moe_experts(h2, block_expert, n_valid, j0, src_rows, dst_rows, expert_w1, expert_w3, expert_w2, l)
        x2 = _combine(x2, y, route, gate2, final_norm_g, seq, final=(l == n_layers - 1))
    return x2.reshape(bsz, seq, d)
```

```python
import functools

import jax
import jax.numpy as jnp
from jax import lax
from jax.experimental import pallas as pl
from jax.experimental.pallas import tpu as pltpu

F32 = jnp.float32
BF16 = jnp.bfloat16

D_MODEL = 2048
GLA_HEADS = 4
GLA_VDIM = 1024
GLA_KDIM = 512
HEAD_K = 128
HEAD_V = 256
DECAY_RANK = 16
GATE_NORMALIZER = 16.0
LOG2_E = 1.4426950408889634
CHUNK = 128
CUMSUM_ROWS = 256
POOL_DIM = 1024
POOL_WINDOWS = (2, 4, 8, 16)
POOL_GROUP_DIM = 256
POOL_HALO = 16
N_GROUPS = 8
EXPERTS_PER_GROUP = 8
N_EXPERTS = 64
TOP_K_FINE = 2
D_FF_EXPERT = 256
MOE_BLOCK = 256
MATMUL_PIECES = 8
ISSUE_UNROLL = 8
GATHER_AHEAD = 2
DMA_PRIORITIES = 2
N_MOD = 6
EPS = 1e-6
LANES = 128
MAIN_COLS = 8192
VMEM_LIMIT = 56 * 1024 * 1024


def _cparams(sem):
    return pltpu.CompilerParams(dimension_semantics=sem, vmem_limit_bytes=VMEM_LIMIT)


def _split2(a):
    hi = a.astype(BF16)
    lo = (a - hi.astype(F32)).astype(BF16)
    return hi, lo


def _split3(a):
    p1 = a.astype(BF16)
    r1 = a - p1.astype(F32)
    p2 = r1.astype(BF16)
    p3 = (r1 - p2.astype(F32)).astype(BF16)
    return p1, p2, p3


def _dot(a, b):
    return jnp.dot(a, b, preferred_element_type=F32)


HIGH_HALF = 0xFFFF0000


def _pack_bf16_pair(lo, hi):
    lo = lax.bitcast_convert_type(lo.astype(BF16).astype(F32), jnp.uint32)
    hi = lax.bitcast_convert_type(hi.astype(BF16).astype(F32), jnp.uint32)
    return (lo >> 16) | (hi & jnp.uint32(HIGH_HALF))


def _pack_bf16_halves(v):
    n = v.shape[1] // 2
    return _pack_bf16_pair(v[:, :n], v[:, n:])


def _unpack_bf16_halves(w):
    lo = lax.bitcast_convert_type(w << 16, F32)
    hi = lax.bitcast_convert_type(w & jnp.uint32(HIGH_HALF), F32)
    return jnp.concatenate([lo, hi], axis=1)


def _dot3(a, b):
    ah, al = _split2(a)
    bh, bl = _split2(b)
    return _dot(ah, bh) + (_dot(ah, bl) + _dot(al, bh))


def _ada_kernel(c_ref, w_ref, b_ref, o_ref):
    c = c_ref[...]
    s = (c * jax.nn.sigmoid(c)).astype(BF16)
    o_ref[0] = _dot(s, w_ref[0].astype(BF16)) + b_ref[0]


def _ada_mod(c, ada_w, ada_b):
    n_layers, d, n = ada_w.shape
    bsz = c.shape[0]
    rows = 8
    tn = 2048
    c_pad = jnp.zeros((rows, d), F32).at[:bsz].set(c)
    out = pl.pallas_call(
        _ada_kernel,
        grid=(n_layers, n // tn),
        in_specs=[
            pl.BlockSpec((rows, d), lambda l, j: (0, 0)),
            pl.BlockSpec((1, d, tn), lambda l, j: (l, 0, j)),
            pl.BlockSpec((1, 1, tn), lambda l, j: (l, 0, j)),
        ],
        out_specs=pl.BlockSpec((1, rows, tn), lambda l, j: (l, 0, j)),
        out_shape=jax.ShapeDtypeStruct((n_layers, rows, n), F32),
        compiler_params=_cparams(("parallel", "parallel")),
        name="ada_mod",
    )(c_pad, ada_w, ada_b.reshape(n_layers, 1, n))
    return out[:, :bsz].reshape(n_layers, bsz, N_MOD, 1, d)


def _inproj_kernel(x_ref, g_ref, sc_ref, sh_ref, w_ref, wlr_ref, o_ref, lr_ref, h_scr):
    @pl.when(pl.program_id(1) == 0)
    def _():
        x = x_ref[...]
        ms = jnp.mean(x * x, axis=-1, keepdims=True)
        y = x * lax.rsqrt(ms + EPS) * g_ref[...]
        hb = (y * (1.0 + sc_ref[0]) + sh_ref[0]).astype(BF16)
        h_scr[...] = hb
        lr_ref[...] = _dot(hb, wlr_ref[0])

    o_ref[...] = _dot(h_scr[...], w_ref[0]).astype(o_ref.dtype)


def _in_proj(x2, g, scale, shift, w_main, w_lr, layer, seq):
    t, d = x2.shape
    tm = min(1024, seq)
    tn = 1024
    tiles_per_seq = seq // tm
    bidx = lambda i, j: (i // tiles_per_seq, 0, 0)
    return pl.pallas_call(
        _inproj_kernel,
        grid=(t // tm, MAIN_COLS // tn),
        in_specs=[
            pl.BlockSpec((tm, d), lambda i, j: (i, 0)),
            pl.BlockSpec((1, d), lambda i, j: (0, 0)),
            pl.BlockSpec((1, 1, d), bidx),
            pl.BlockSpec((1, 1, d), bidx),
            pl.BlockSpec((1, d, tn), lambda i, j: (layer, 0, j)),
            pl.BlockSpec((1, d, LANES), lambda i, j: (layer, 0, 0)),
        ],
        out_specs=[
            pl.BlockSpec((tm, tn), lambda i, j: (i, j)),
            pl.BlockSpec((tm, LANES), lambda i, j: (i, 0)),
        ],
        out_shape=[
            jax.ShapeDtypeStruct((t, MAIN_COLS), BF16),
            jax.ShapeDtypeStruct((t, LANES), F32),
        ],
        scratch_shapes=[pltpu.VMEM((tm, d), BF16)],
        compiler_params=_cparams(("parallel", "arbitrary")),
        name="in_proj",
    )(x2, g.reshape(1, d), scale, shift, w_main, w_lr)


def _gla_kernel(*refs, reverse, bt):
    if reverse:
        q_ref, k_ref, v_ref, lr_ref, wd_ref, bd_ref, o_ref, st_ref = refs
    else:
        (q_ref, k_ref, v_ref, lr_ref, wd_ref, bd_ref, ob_ref, r_ref, gn_ref,
         o_ref, st_ref, o_scr) = refs

    @pl.when(pl.program_id(1) == 0)
    def _():
        st_ref[...] = jnp.zeros_like(st_ref)

    z = _dot3(lr_ref[...], wd_ref[...]) + bd_ref[...]
    la = (jnp.minimum(z, 0.0) - jnp.log(1.0 + jnp.exp(-jnp.abs(z)))) * (LOG2_E / GATE_NORMALIZER)

    tb = min(bt, CUMSUM_ROWS)
    rows = lax.broadcasted_iota(jnp.int32, (tb, tb), 0)
    cols = lax.broadcasted_iota(jnp.int32, (tb, tb), 1)
    same_chunk = (rows // CHUNK) == (cols // CHUNK)
    ordered = (cols >= rows) if reverse else (cols <= rows)
    tri = jnp.where(same_chunk & ordered, 1.0, 0.0).astype(BF16)
    l1, l2, l3 = _split3(la)
    cum = jnp.concatenate(
        [_dot(tri, l1[s:s + tb]) + (_dot(tri, l2[s:s + tb]) + _dot(tri, l3[s:s + tb])) for s in range(0, bt, tb)],
        axis=0)

    crow = lax.broadcasted_iota(jnp.int32, (CHUNK, CHUNK), 0)
    ccol = lax.broadcasted_iota(jnp.int32, (CHUNK, CHUNK), 1)
    att_mask = (ccol > crow) if reverse else (ccol <= crow)

    n_chunks = bt // CHUNK
    chunk_order = range(n_chunks - 1, -1, -1) if reverse else range(n_chunks)
    q_scale = HEAD_K ** -0.5
    for c in chunk_order:
        sl = slice(c * CHUNK, (c + 1) * CHUNK)
        for h in range(GLA_HEADS):
            hk = slice(h * HEAD_K, (h + 1) * HEAD_K)
            hv = slice(h * HEAD_V, (h + 1) * HEAD_V)
            state = st_ref[h]
            cm = cum[sl, hk]
            last = cm[0:1] if reverse else cm[CHUNK - 1:CHUNK]
            mid = cm[CHUNK // 2:CHUNK // 2 + 1]
            qf = q_ref[sl, hk].astype(F32) * q_scale
            kf = k_ref[sl, hk].astype(F32)
            vb = v_ref[sl, hv]
            q_in = (qf * jnp.exp2(cm)).astype(BF16)
            q_att = (qf * jnp.exp2(cm - mid)).astype(BF16)
            k_att = (kf * jnp.exp2(mid - cm)).astype(BF16)
            k_end = (kf * jnp.exp2(last - cm)).astype(BF16)
            att = lax.dot_general(q_att, k_att, (((1,), (1,)), ((), ())), preferred_element_type=F32)
            att = jnp.where(att_mask, att, 0.0).astype(BF16)
            o = _dot(att, vb) + _dot(q_in, state.astype(BF16))
            if reverse:
                o_ref[sl, hv] = o
            else:
                o_scr[sl, hv] = o
            dec_col = jnp.transpose(jnp.broadcast_to(jnp.exp2(last), (HEAD_K, HEAD_K)))
            dec = jnp.concatenate([dec_col, dec_col], axis=1)
            kv = lax.dot_general(k_end, vb, (((0,), (0,)), ((), ())), preferred_element_type=F32)
            st_ref[h] = dec * state + kv

    if not reverse:
        for h in range(GLA_HEADS):
            hv = slice(h * HEAD_V, (h + 1) * HEAD_V)
            o = o_scr[:, hv] + ob_ref[:, hv]
            ms = jnp.mean(o * o, axis=-1, keepdims=True)
            y = o * lax.rsqrt(ms + EPS) * gn_ref[...]
            r = r_ref[:, hv].astype(F32)
            o_ref[:, hv] = (y * (r * jax.nn.sigmoid(r))).astype(o_ref.dtype)


def _gla_scan(main, lr, wdec, bdec, seq, reverse, o_bwd=None, g_norm=None):
    t = main.shape[0]
    bsz = t // seq
    bt = min(512, seq)
    nb = seq // bt

    def row(b, n):
        return b * nb + ((nb - 1 - n) if reverse else n)

    in_specs = [
        pl.BlockSpec((bt, GLA_KDIM), lambda b, n: (row(b, n), 0)),
        pl.BlockSpec((bt, GLA_KDIM), lambda b, n: (row(b, n), 1)),
        pl.BlockSpec((bt, GLA_VDIM), lambda b, n: (row(b, n), 1)),
        pl.BlockSpec((bt, LANES), lambda b, n: (row(b, n), 0)),
        pl.BlockSpec((LANES, GLA_KDIM), lambda b, n: (0, 0)),
        pl.BlockSpec((1, GLA_KDIM), lambda b, n: (0, 0)),
    ]
    args = [main, main, main, lr, wdec, bdec]
    scratch = [pltpu.VMEM((GLA_HEADS, HEAD_K, HEAD_V), F32)]
    if reverse:
        out_dtype = F32
    else:
        in_specs += [
            pl.BlockSpec((bt, GLA_VDIM), lambda b, n: (row(b, n), 0)),
            pl.BlockSpec((bt, GLA_VDIM), lambda b, n: (row(b, n), 2)),
            pl.BlockSpec((1, HEAD_V), lambda b, n: (0, 0)),
        ]
        args += [o_bwd, main, g_norm.reshape(1, HEAD_V)]
        scratch.append(pltpu.VMEM((bt, GLA_VDIM), F32))
        out_dtype = BF16
    return pl.pallas_call(
        functools.partial(_gla_kernel, reverse=reverse, bt=bt),
        grid=(bsz, nb),
        in_specs=in_specs,
        out_specs=pl.BlockSpec((bt, GLA_VDIM), lambda b, n: (row(b, n), 0)),
        out_shape=jax.ShapeDtypeStruct((t, GLA_VDIM), out_dtype),
        scratch_shapes=scratch,
        compiler_params=_cparams(("parallel", "arbitrary")),
        name="gla_bwd" if reverse else "gla_fwd",
    )(*args)


def _pool_tile(up_ref, uc_ref, un_ref, w_ref, b_ref, s_ref, ext_ref, *, tm, tiles_per_seq, seq):
    it = pl.program_id(0) % tiles_per_seq
    prev = up_ref[...].astype(F32)
    nxt = un_ref[...].astype(F32)
    ext_ref[0:POOL_HALO, :] = jnp.where(it == 0, 0.0, prev)
    ext_ref[POOL_HALO:POOL_HALO + tm, :] = uc_ref[...].astype(F32)
    ext_ref[POOL_HALO + tm:, :] = jnp.where(it == tiles_per_seq - 1, 0.0, nxt)
    pos = it * tm + lax.broadcasted_iota(jnp.int32, (tm, POOL_GROUP_DIM), 0)
    outs = []
    for gi, w in enumerate(POOL_WINDOWS):
        cs = slice(gi * POOL_GROUP_DIM, (gi + 1) * POOL_GROUP_DIM)
        start = POOL_HALO - w // 2
        acc = ext_ref[start:start + tm, cs]
        for j in range(1, w):
            acc = acc + ext_ref[start + j:start + j + tm, cs]
        cnt = jnp.minimum(pos + w // 2, seq) - jnp.maximum(pos - w // 2, 0)
        p = acc / cnt.astype(F32) - ext_ref[POOL_HALO:POOL_HALO + tm, cs]
        y = _dot(p.astype(BF16), w_ref[0, gi])
        outs.append(((y + b_ref[0, :, cs]) * s_ref[0, :, cs]).astype(BF16))
    return jnp.concatenate(outs, axis=1)


def _merge_kernel(a_ref, up_ref, uc_ref, un_ref, pw_ref, pb_ref, ps_ref, ga_ref, gb_ref, x_ref,
                  wa_ref, wb_ref, wo_ref, g1_ref, fg_ref, sc_ref, sh_ref, wr_ref, br_ref,
                  xo_ref, h_ref, rt_ref, ext_ref, *, tm, tiles_per_seq, seq):
    b_in = _pool_tile(up_ref, uc_ref, un_ref, pw_ref, pb_ref, ps_ref, ext_ref,
                      tm=tm, tiles_per_seq=tiles_per_seq, seq=seq)
    ya = _dot(a_ref[...], wa_ref[0])
    yb = _dot(b_in, wb_ref[0])
    merged = jax.nn.sigmoid(ga_ref[...].astype(F32)) * ya + jax.nn.sigmoid(gb_ref[...].astype(F32)) * yb
    xn = x_ref[...] + g1_ref[0] * _dot(merged.astype(BF16), wo_ref[0])
    xo_ref[...] = xn
    ms = jnp.mean(xn * xn, axis=-1, keepdims=True)
    h = xn * lax.rsqrt(ms + EPS) * fg_ref[...] * (1.0 + sc_ref[0]) + sh_ref[0]
    h_ref[...] = _pack_bf16_halves(h)

    hh, hl = _split2(h)
    both = _dot(hh, wr_ref[0])
    lg = both[:, :LANES] + (both[:, LANES:] + _dot(hl, wr_ref[0, :, :LANES])) + br_ref[0]
    lane = lax.broadcasted_iota(jnp.int32, lg.shape, 1).astype(F32)
    neg = -jnp.inf
    far = float(LANES)
    coarse = lane < N_GROUPS
    cmax = jnp.max(jnp.where(coarse, lg, neg), axis=-1, keepdims=True)
    g_star = jnp.min(jnp.where(coarse & (lg == cmax), lane, far), axis=-1, keepdims=True)
    p_g = 1.0 / jnp.sum(jnp.where(coarse, jnp.exp(lg - cmax), 0.0), axis=-1, keepdims=True)
    lo = N_GROUPS + EXPERTS_PER_GROUP * g_star
    fine = (lane >= lo) & (lane < lo + EXPERTS_PER_GROUP)
    v1 = jnp.max(jnp.where(fine, lg, neg), axis=-1, keepdims=True)
    i1 = jnp.min(jnp.where(fine & (lg == v1), lane, far), axis=-1, keepdims=True)
    fine2 = fine & (lane != i1)
    v2 = jnp.max(jnp.where(fine2, lg, neg), axis=-1, keepdims=True)
    i2 = jnp.min(jnp.where(fine2 & (lg == v2), lane, far), axis=-1, keepdims=True)
    e = jnp.exp(v2 - v1)
    w1 = p_g / (1.0 + e)
    w2 = p_g * e / (1.0 + e)
    rt = jnp.where(lane == 0, i1 - N_GROUPS,
                   jnp.where(lane == 1, i2 - N_GROUPS,
                             jnp.where(lane == 2, w1, jnp.where(lane == 3, w2, 0.0))))
    rt_ref[...] = rt


def _merge(a_in, main, x2, pool_w, pool_b, pool_s, wa, wb, wo, gate1, fg, scale2, shift2, wr, br,
           layer, seq):
    t, d = x2.shape
    tm = min(256, seq)
    tiles_per_seq = seq // tm
    const = dict(pipeline_mode=pl.Buffered(1))
    bidx = lambda i: (i // tiles_per_seq, 0, 0)
    hb = tm // POOL_HALO
    n_hblocks = t // POOL_HALO
    u_col = 3
    n_win = len(POOL_WINDOWS)
    return pl.pallas_call(
        functools.partial(_merge_kernel, tm=tm, tiles_per_seq=tiles_per_seq, seq=seq),
        grid=(t // tm,),
        in_specs=[
            pl.BlockSpec((tm, GLA_VDIM), lambda i: (i, 0)),
            pl.BlockSpec((POOL_HALO, POOL_DIM), lambda i: (jnp.maximum(i * hb - 1, 0), u_col)),
            pl.BlockSpec((tm, POOL_DIM), lambda i: (i, u_col)),
            pl.BlockSpec((POOL_HALO, POOL_DIM), lambda i: (jnp.minimum((i + 1) * hb, n_hblocks - 1), u_col)),
            pl.BlockSpec((1, n_win, POOL_GROUP_DIM, POOL_GROUP_DIM), lambda i: (layer, 0, 0, 0)),
            pl.BlockSpec((1, 1, POOL_DIM), lambda i: (layer, 0, 0)),
            pl.BlockSpec((1, 1, POOL_DIM), lambda i: (layer, 0, 0)),
            pl.BlockSpec((tm, d), lambda i: (i, 2)),
            pl.BlockSpec((tm, d), lambda i: (i, 3)),
            pl.BlockSpec((tm, d), lambda i: (i, 0)),
            pl.BlockSpec((1, GLA_VDIM, d), lambda i: (layer, 0, 0), **const),
            pl.BlockSpec((1, POOL_DIM, d), lambda i: (layer, 0, 0), **const),
            pl.BlockSpec((1, d, d), lambda i: (layer, 0, 0), **const),
            pl.BlockSpec((1, 1, d), bidx),
            pl.BlockSpec((1, d), lambda i: (0, 0)),
            pl.BlockSpec((1, 1, d), bidx),
            pl.BlockSpec((1, 1, d), bidx),
            pl.BlockSpec((1, d, 2 * LANES), lambda i: (layer, 0, 0), **const),
            pl.BlockSpec((1, 1, LANES), lambda i: (layer, 0, 0)),
        ],
        out_specs=[
            pl.BlockSpec((tm, d), lambda i: (i, 0)),
            pl.BlockSpec((tm, d // 2), lambda i: (i, 0)),
            pl.BlockSpec((tm, LANES), lambda i: (i, 0)),
        ],
        out_shape=[
            jax.ShapeDtypeStruct((t, d), F32),
            jax.ShapeDtypeStruct((t, d // 2), jnp.uint32),
            jax.ShapeDtypeStruct((t, LANES), F32),
        ],
        scratch_shapes=[pltpu.VMEM((tm + 2 * POOL_HALO, POOL_DIM), F32)],
        compiler_params=_cparams(("parallel",)),
        name="merge",
    )(a_in, main, main, main, pool_w, pool_b, pool_s, main, main, x2, wa, wb, wo, gate1, fg.reshape(1, d),
      scale2, shift2, wr, br)


def _moe_kernel(be_ref, nv_ref, j0_ref, src_ref, dst_ref, h_hbm, w1_ref, w3_ref, w2_ref, y_hbm,
                xbuf, ybuf, w1b, w3b, w2b, wview, gsem, ssem, *, n_blocks):
    b = pl.program_id(0)
    slot = b % 2

    def gather_copy(sl, src_row, g, u):
        return pltpu.make_async_copy(h_hbm.at[pl.ds(src_row, 1)], xbuf.at[sl, g, pl.ds(u, 1)], gsem.at[sl])

    def scatter_copy(sl, g, u, dst_row):
        return pltpu.make_async_copy(ybuf.at[sl, g, pl.ds(u, 1)], y_hbm.at[pl.ds(dst_row, 1)], ssem.at[sl])

    def issue_gather_group(base, sl, g):
        for u in range(ISSUE_UNROLL):
            gather_copy(sl, src_ref[base + g * ISSUE_UNROLL + u], g, u).start(priority=u % DMA_PRIORITIES)

    def issue_scatter(blk, sl):
        base = j0_ref[blk]
        n = nv_ref[blk]
        full = n // ISSUE_UNROLL

        def group(g, carry):
            for u in range(ISSUE_UNROLL):
                scatter_copy(sl, g, u, dst_ref[base + g * ISSUE_UNROLL + u]).start(priority=u % DMA_PRIORITIES)
            return carry
        lax.fori_loop(0, full, group, 0)

        def tail(i, carry):
            scatter_copy(sl, full, i - full * ISSUE_UNROLL, dst_ref[base + i]).start()
            return carry
        lax.fori_loop(full * ISSUE_UNROLL, n, tail, 0)

    def wait_rows(sem, n):
        pltpu.make_async_copy(wview.at[pl.ds(0, n)], wview.at[pl.ds(0, n)], sem).wait()

    def wait_gather(sl):
        wait_rows(gsem.at[sl], MOE_BLOCK)

    def wait_scatter(blk, sl):
        wait_rows(ssem.at[sl], nv_ref[blk])

    n_groups = MOE_BLOCK // ISSUE_UNROLL
    used = nv_ref[b] > 0
    prev_used = nv_ref[jnp.maximum(b - 1, 0)] > 0
    prev2_used = nv_ref[jnp.maximum(b - 2, 0)] > 0
    xslot = b % (GATHER_AHEAD + 1)
    ahead_slot = (b + GATHER_AHEAD) % (GATHER_AHEAD + 1)

    @pl.when(b == 0)
    def _():
        for blk in range(GATHER_AHEAD):
            def first(g, carry, blk=blk):
                issue_gather_group(j0_ref[blk], blk, g)
                return carry
            lax.fori_loop(0, n_groups, first, 0)

    @pl.when((b >= 2) & prev2_used)
    def _():
        wait_scatter(b - 2, slot)

    @pl.when(used)
    def _():
        prev_e = be_ref[jnp.maximum(b - 1, 0)]

        @pl.when((b == 0) | (be_ref[b] != prev_e))
        def _():
            w1b[...] = w1_ref[0, 0].astype(BF16)
            w3b[...] = w3_ref[0, 0].astype(BF16)
            w2b[...] = w2_ref[0, 0].astype(BF16)

        wait_gather(xslot)
        next_base = j0_ref[jnp.minimum(b + GATHER_AHEAD, n_blocks - 1)]
        dh = xbuf.shape[-1]
        pieces = MATMUL_PIECES
        kc = 2 * dh // pieces
        per_piece = n_groups // (2 * pieces)
        xb = _unpack_bf16_halves(xbuf[xslot].reshape(MOE_BLOCK, dh)).astype(BF16)
        a = g = None
        for k in range(pieces):
            ks = slice(k * kc, (k + 1) * kc)
            pa = _dot(xb[:, ks], w1b[ks, :])
            pg = _dot(xb[:, ks], w3b[ks, :])
            a = pa if a is None else a + pa
            g = pg if g is None else g + pg
            for j in range(per_piece):
                issue_gather_group(next_base, ahead_slot, k * per_piece + j)
        mid = (a * jax.nn.sigmoid(a) * g).astype(BF16)
        for n in range(pieces // 2):
            ns = slice(n * kc, (n + 1) * kc)
            ns_hi = slice(dh + n * kc, dh + (n + 1) * kc)
            packed = _pack_bf16_pair(_dot(mid, w2b[:, ns]), _dot(mid, w2b[:, ns_hi]))
            ybuf[slot, :, :, ns] = packed.reshape(n_groups, ISSUE_UNROLL, kc)
            for j in range(2 * per_piece):
                issue_gather_group(next_base, ahead_slot, (pieces + 2 * n) * per_piece + j)
        issue_scatter(b, slot)

    @pl.when(jnp.logical_not(used) & ((b < GATHER_AHEAD) | prev2_used))
    def _():
        wait_gather(xslot)

    @pl.when(b == n_blocks - 1)
    def _():
        @pl.when(prev_used)
        def _():
            wait_gather((b + 1) % (GATHER_AHEAD + 1))

        @pl.when(used)
        def _():
            wait_gather(ahead_slot)

        @pl.when((b >= 1) & prev_used)
        def _():
            wait_scatter(b - 1, 1 - slot)

        @pl.when(used)
        def _():
            wait_scatter(b, slot)


def _moe_experts(h2, block_expert, n_valid, j0, src_rows, dst_rows, w1, w3, w2, layer):
    t, dh = h2.shape
    d = 2 * dh
    n_blocks = block_expert.shape[0]
    f = w1.shape[-1]
    row_buf = lambda n: pltpu.VMEM((n, MOE_BLOCK // ISSUE_UNROLL, ISSUE_UNROLL, dh), jnp.uint32)
    grid_spec = pltpu.PrefetchScalarGridSpec(
        num_scalar_prefetch=5,
        grid=(n_blocks,),
        in_specs=[
            pl.BlockSpec(memory_space=pl.ANY),
            pl.BlockSpec((1, 1, d, f), lambda b, be, nv, j0, sr, ds: (layer, be[b], 0, 0)),
            pl.BlockSpec((1, 1, d, f), lambda b, be, nv, j0, sr, ds: (layer, be[b], 0, 0)),
            pl.BlockSpec((1, 1, f, d), lambda b, be, nv, j0, sr, ds: (layer, be[b], 0, 0)),
        ],
        out_specs=pl.BlockSpec(memory_space=pl.ANY),
        scratch_shapes=[
            row_buf(GATHER_AHEAD + 1),
            row_buf(2),
            pltpu.VMEM((d, f), BF16),
            pltpu.VMEM((d, f), BF16),
            pltpu.VMEM((f, d), BF16),
            pltpu.VMEM((MOE_BLOCK, dh // LANES, LANES), jnp.uint32),
            pltpu.SemaphoreType.DMA((GATHER_AHEAD + 1,)),
            pltpu.SemaphoreType.DMA((2,)),
        ],
    )
    return pl.pallas_call(
        functools.partial(_moe_kernel, n_blocks=n_blocks),
        grid_spec=grid_spec,
        out_shape=jax.ShapeDtypeStruct((TOP_K_FINE * t, dh), jnp.uint32),
        compiler_params=_cparams(("arbitrary",)),
        name="moe_experts",
    )(block_expert, n_valid, j0, src_rows, dst_rows, h2, w1, w3, w2)


def _route_plan(route, t):
    n_assign = t * TOP_K_FINE
    n_blocks = -(-n_assign // MOE_BLOCK) + N_EXPERTS
    e_flat = route[:, :TOP_K_FINE].astype(jnp.int32).reshape(-1)
    eids = jnp.arange(N_EXPERTS, dtype=jnp.int32)
    counts = jnp.sum((e_flat[:, None] == eids[None, :]).astype(jnp.int32), axis=0)
    start = jnp.cumsum(counts) - counts
    nblk = (counts + MOE_BLOCK - 1) // MOE_BLOCK
    bend = jnp.cumsum(nblk)
    bstart = bend - nblk
    order = jnp.argsort(e_flat, stable=True).astype(jnp.int32)
    blocks = jnp.arange(n_blocks, dtype=jnp.int32)[:, None]
    owner = ((blocks >= bstart[None, :]) & (blocks < bend[None, :])).astype(jnp.int32)
    block_expert = jnp.minimum(jnp.sum((blocks >= bend[None, :]).astype(jnp.int32), axis=1), N_EXPERTS - 1)
    in_expert = blocks[:, 0] - jnp.sum(owner * bstart[None, :], axis=1)
    n_valid = jnp.clip(jnp.sum(owner * counts[None, :], axis=1) - in_expert * MOE_BLOCK, 0, MOE_BLOCK)
    j0 = jnp.minimum(jnp.sum(owner * start[None, :], axis=1) + in_expert * MOE_BLOCK, n_assign)
    pad = jnp.zeros((MOE_BLOCK,), jnp.int32)
    src_rows = jnp.concatenate([order // TOP_K_FINE, pad])
    dst_rows = jnp.concatenate([(order % TOP_K_FINE) * t + order // TOP_K_FINE, pad])
    return block_expert, n_valid, j0, src_rows, dst_rows


def _combine_kernel(x_ref, y0_ref, y1_ref, rt_ref, g2_ref, fg_ref, o_ref, *, final):
    rt = rt_ref[...]
    y0 = _unpack_bf16_halves(y0_ref[...])
    y1 = _unpack_bf16_halves(y1_ref[...])
    xn = x_ref[...] + g2_ref[0] * (rt[:, 2:3] * y0 + rt[:, 3:4] * y1)
    if final:
        ms = jnp.mean(xn * xn, axis=-1, keepdims=True)
        xn = xn * lax.rsqrt(ms + EPS) * fg_ref[...]
    o_ref[...] = xn


def _combine(x2, y, route, gate2, final_g, seq, final):
    t, d = x2.shape
    tm = min(512, seq)
    tiles_per_seq = seq // tm
    n_tiles = t // tm
    return pl.pallas_call(
        functools.partial(_combine_kernel, final=final),
        grid=(n_tiles,),
        in_specs=[
            pl.BlockSpec((tm, d), lambda i: (i, 0)),
            pl.BlockSpec((tm, d // 2), lambda i: (i, 0)),
            pl.BlockSpec((tm, d // 2), lambda i: (i + n_tiles, 0)),
            pl.BlockSpec((tm, LANES), lambda i: (i, 0)),
            pl.BlockSpec((1, 1, d), lambda i: (i // tiles_per_seq, 0, 0)),
            pl.BlockSpec((1, d), lambda i: (0, 0)),
        ],
        out_specs=pl.BlockSpec((tm, d), lambda i: (i, 0)),
        out_shape=jax.ShapeDtypeStruct((t, d), F32),
        compiler_params=_cparams(("parallel",)),
        name="combine_final" if final else "combine",
    )(x2, y, y, route, gate2, final_g.reshape(1, d))


def _prep_in_w(in_w):
    lr0 = 2 * GLA_KDIM + 2 * GLA_VDIM
    lr1 = lr0 + 2 * DECAY_RANK
    w_main = jnp.concatenate([in_w[:, :, :lr0].astype(BF16), in_w[:, :, lr1:].astype(BF16)], axis=2)
    w_lr = jnp.pad(in_w[:, :, lr0:lr1].astype(BF16), ((0, 0), (0, 0), (0, LANES - 2 * DECAY_RANK)))
    return w_main, w_lr


def _prep_router(wc, bc, wf, bf):
    pad = LANES - N_GROUPS - N_EXPERTS
    wr = jnp.pad(jnp.concatenate([wc, wf], axis=2), ((0, 0), (0, 0), (0, pad)))
    wr_hi = wr.astype(BF16)
    wr_lo = (wr - wr_hi.astype(F32)).astype(BF16)
    br = jnp.pad(jnp.concatenate([bc, bf], axis=1), ((0, 0), (0, pad)))[:, None, :]
    return jnp.concatenate([wr_hi, wr_lo], axis=2), br


def kernel(x, c, ada_w, ada_b, mix_norm_g, in_w, decay_fw_w, decay_fw_b, decay_bw_w, decay_bw_b, gla_norm_g, pool_w, pool_b, pool_scale, branch_a_w, branch_b_w, out_w, ffn_norm_g, router_coarse_w, router_coarse_b, router_fine_w, router_fine_b, expert_w1, expert_w3, expert_w2, final_norm_g):
    bsz, seq, d = x.shape
    t = bsz * seq
    n_layers = ada_w.shape[0]
    mod = _ada_mod(c, ada_w, ada_b)
    w_main, w_lr = _prep_in_w(in_w)
    wr, br = _prep_router(router_coarse_w, router_coarse_b, router_fine_w, router_fine_b)
    wa, wb, wo = branch_a_w.astype(BF16), branch_b_w.astype(BF16), out_w.astype(BF16)
    rank_pad = ((0, 0), (0, LANES - 2 * DECAY_RANK), (0, 0))
    wdec_f = jnp.pad(jnp.concatenate([decay_fw_w, jnp.zeros_like(decay_fw_w)], axis=1), rank_pad)
    wdec_b = jnp.pad(jnp.concatenate([jnp.zeros_like(decay_bw_w), decay_bw_w], axis=1), rank_pad)
    pool_wb = pool_w.astype(BF16)
    pool_b3, pool_s3 = pool_b[:, None, :], pool_scale[:, None, :]

    x2 = x.reshape(t, d)
    for l in range(n_layers):
        shift1, scale1, gate1, shift2, scale2, gate2 = [mod[l, :, i] for i in range(N_MOD)]
        main, lr = _in_proj(x2, mix_norm_g[l], scale1, shift1, w_main, w_lr, l, seq)
        o_bwd = _gla_scan(main, lr, wdec_b[l], decay_bw_b[l].reshape(1, GLA_KDIM), seq, reverse=True)
        a_in = _gla_scan(main, lr, wdec_f[l], decay_fw_b[l].reshape(1, GLA_KDIM), seq, reverse=False,
                         o_bwd=o_bwd, g_norm=gla_norm_g[l])
        x2, h2, route = _merge(a_in, main, x2, pool_wb, pool_b3, pool_s3, wa, wb, wo, gate1, ffn_norm_g[l],
                               scale2, shift2, wr, br, l, seq)
        block_expert, n_valid, j0, src_rows, dst_rows = _route_plan(route, t)
        y = _moe_experts(h2, block_expert, n_valid, j0, src_rows, dst_rows, expert_w1, expert_w3, expert_w2, l)
        x2 = _combine(x2, y, route, gate2, final_norm_g, seq, final=(l == n_layers - 1))
    return x2.reshape(bsz, seq, d)
```
